```python
import math
import jax, jax.numpy as jnp
from jax import lax
import numpy as np

D_MODEL = 1024
BATCH = 8
SEQ = 2048
DEPTH = 1
DEC_BATCH = 128
DEC_SEQ = 8
PAST_LEN = 16384
PAGE_SIZE = 128

N_META = 16
CHUNK = 64
CONV_W = 4
RMS_EPS = 1e-6

GDN_HEADS = 4
GDN_DK = 128
GDN_DV = 128
GDN_WIDTH = GDN_HEADS * GDN_DV
GDN_CONV_DIM = GDN_HEADS * (2 * GDN_DK + GDN_DV)

MLSTM_HEADS = 4
MLSTM_DK = 128
MLSTM_DV = 128
MLSTM_WIDTH = MLSTM_HEADS * MLSTM_DV
MLSTM_CONV_DIM = 2 * MLSTM_HEADS * MLSTM_DK

IN_SPLITS = (GDN_HEADS * GDN_DK, GDN_HEADS * GDN_DK, GDN_HEADS * GDN_DV, GDN_HEADS, GDN_HEADS, GDN_HEADS * GDN_DV,
             MLSTM_HEADS * MLSTM_DK, MLSTM_HEADS * MLSTM_DK, MLSTM_HEADS * MLSTM_DV, MLSTM_HEADS, MLSTM_HEADS, MLSTM_HEADS * MLSTM_DV,
             D_MODEL, D_MODEL)
N_IN = 2 * (GDN_HEADS * (2 * GDN_DK + 2 * GDN_DV + 2)) + 2 * D_MODEL

PEER_HEADS = 8
N_KEYS = 128
N_EXPERTS = N_KEYS * N_KEYS
PEER_TOPK = 16
PEER_DQ = 256
KEY_HALF = PEER_DQ // 2
PEER_BLOCK = 128

kernel_name = 'hybrid_gdn_mlstm_peer_step'


def rms_norm(x, gain):
    xf = x.astype(jnp.float32)
    y = xf * lax.rsqrt(jnp.mean(xf * xf, axis=-1, keepdims=True) + RMS_EPS)
    return (y * gain.astype(jnp.float32)).astype(x.dtype)


def l2norm(x):
    return x * lax.rsqrt(jnp.sum(x * x, axis=-1, keepdims=True) + RMS_EPS)


def split_cols(t, sizes):
    offs = np.cumsum(np.array(sizes))[:-1].tolist()
    return jnp.split(t, offs, axis=-1)


def heads(t, n):
    b, l, _ = t.shape
    return t.reshape(b, l, n, -1).transpose(0, 2, 1, 3).astype(jnp.float32)


def causal_conv(ext, w):
    L = ext.shape[1] - (CONV_W - 1)
    y = ext[:, 0:L] * w[0]
    for j in range(1, CONV_W):
        y = y + ext[:, j:j + L] * w[j]
    return y


def gdn_chunked(q, k, v, g, beta, S0, chunk):
    B, H, L, _ = q.shape
    nc = L // chunk

    def blocks(t):
        return t.reshape(B, H, nc, chunk, *t.shape[3:])

    q, k, v, g, beta = blocks(q), blocks(k), blocks(v), blocks(g), blocks(beta)
    G = jnp.cumsum(g, axis=-1)
    incl = jnp.tril(jnp.ones((chunk, chunk), dtype=bool))
    strict = jnp.tril(jnp.ones((chunk, chunk), dtype=bool), k=-1)
    decay = jnp.exp(jnp.where(incl, G[..., :, None] - G[..., None, :], -jnp.inf))
    kk = jnp.einsum('bhnck,bhnsk->bhncs', k, k)
    lhs = jnp.where(strict, beta[..., :, None] * decay * kk, 0.0) + jnp.eye(chunk, dtype=jnp.float32)

    def solve(rhs):
        return lax.linalg.triangular_solve(lhs, rhs, left_side=True, lower=True, unit_diagonal=True)

    u = solve(beta[..., None] * v)
    w = solve((beta * jnp.exp(G))[..., None] * k)
    qk = jnp.einsum('bhnck,bhnsk->bhncs', q, k) * decay
    q_dec = q * jnp.exp(G)[..., None]
    k_end = k * jnp.exp(G[..., -1:] - G)[..., None]
    g_end = jnp.exp(G[..., -1])
    xs = tuple(jnp.moveaxis(t, 2, 0) for t in (u, w, qk, q_dec, k_end, g_end))

    def step(S, inp):
        u_c, w_c, qk_c, qd_c, ke_c, ge_c = inp
        delta = u_c - jnp.einsum('bhck,bhvk->bhcv', w_c, S)
        o = jnp.einsum('bhck,bhvk->bhcv', qd_c, S) + jnp.einsum('bhcs,bhsv->bhcv', qk_c, delta)
        S = ge_c[..., None, None] * S + jnp.einsum('bhsv,bhsk->bhvk', delta, ke_c)
        return S, o

    S, o = lax.scan(step, S0, xs)
    return jnp.moveaxis(o, 0, 2).reshape(B, H, L, -1), S


def mlstm_chunked(q, k, v, logi, logf, C0, n0, m0, chunk):
    B, H, L, _ = q.shape
    nc = L // chunk

    def blocks(t):
        return jnp.moveaxis(t.reshape(B, H, nc, chunk, *t.shape[3:]), 2, 0)

    q, k, v, logi, logf = blocks(q), blocks(k), blocks(v), blocks(logi), blocks(logf)
    F = jnp.cumsum(logf, axis=-1)
    incl = jnp.tril(jnp.ones((chunk, chunk), dtype=bool))
    Dmat = jnp.where(incl, F[..., :, None] - F[..., None, :] + logi[..., None, :], -jnp.inf)
    Dmax = jnp.max(Dmat, axis=-1)
    qk = jnp.einsum('nbhck,nbhsk->nbhcs', q, k)
    F_end = F[..., -1]
    E = F_end[..., None] - F + logi
    E_max = jnp.max(E, axis=-1)

    def step(carry, inp):
        Cs, ns, ms = carry
        q_c, k_c, v_c, F_c, D_c, Dmx, qk_c, Fe, E_c, Emx = inp
        m_inter = F_c + ms[..., None]
        m_t = jnp.maximum(m_inter, Dmx)
        w_intra = jnp.exp(D_c - m_t[..., None]) * qk_c
        w_inter = jnp.exp(m_inter - m_t)
        num = w_inter[..., None] * jnp.einsum('bhck,bhvk->bhcv', q_c, Cs) + jnp.einsum('bhcs,bhsv->bhcv', w_intra, v_c)
        den = w_inter * jnp.einsum('bhck,bhk->bhc', q_c, ns) + jnp.sum(w_intra, axis=-1)
        h = num / jnp.maximum(jnp.abs(den), jnp.exp(-m_t))[..., None]
        m_new = jnp.maximum(Fe + ms, Emx)
        dec = jnp.exp(Fe + ms - m_new)
        wk = jnp.exp(E_c - m_new[..., None])
        Cs = dec[..., None, None] * Cs + jnp.einsum('bhs,bhsv,bhsk->bhvk', wk, v_c, k_c)
        ns = dec[..., None] * ns + jnp.einsum('bhs,bhsk->bhk', wk, k_c)
        return (Cs, ns, m_new), h

    (C, n, m), h = lax.scan(step, (C0, n0, m0), (q, k, v, F, Dmat, Dmax, qk, F_end, E, E_max))
    return jnp.moveaxis(h, 0, 2).reshape(B, H, L, -1), C, n, m


def token_mixers(xn, S0, buf_gdn, C0, n0, m0, buf_mlstm, w_in, conv_gdn, gdn_a_log, gdn_dt_bias, gdn_out_norm,
                 conv_mlstm, mlstm_i_bias, mlstm_f_bias, mlstm_out_norm, w_branch, w_out, n_pad, chunk):
    B, L, _ = xn.shape
    dt = xn.dtype
    f32 = jnp.float32
    proj = jnp.einsum('bld,de->ble', xn, w_in)
    qa, ka, va, a, b, za, qb, kb, vb, ib, fb, ob, ga, gb = split_cols(proj, IN_SPLITS)

    def pad_front(t, value=0.0):
        return jnp.pad(t, [(0, 0), (0, 0), (n_pad, 0)] + [(0, 0)] * (t.ndim - 3), constant_values=value)

    ext_a = jnp.concatenate([buf_gdn.astype(dt), jnp.concatenate([qa, ka, va], axis=-1)], axis=1)
    new_buf_gdn = ext_a[:, L:]
    qkv_a = jax.nn.silu(causal_conv(ext_a, conv_gdn))
    qa, ka, va = jnp.split(qkv_a, [GDN_HEADS * GDN_DK, 2 * GDN_HEADS * GDN_DK], axis=-1)
    qa = l2norm(heads(qa, GDN_HEADS)) * (GDN_DK ** -0.5)
    ka = l2norm(heads(ka, GDN_HEADS))
    va = heads(va, GDN_HEADS)
    g = -jnp.exp(gdn_a_log.astype(f32)) * jax.nn.softplus(a.astype(f32) + gdn_dt_bias.astype(f32))
    beta = jax.nn.sigmoid(b.astype(f32))
    o_a, S_new = gdn_chunked(pad_front(qa), pad_front(ka), pad_front(va), pad_front(g.transpose(0, 2, 1)),
                             pad_front(beta.transpose(0, 2, 1)), S0.astype(f32), chunk)
    o_a = o_a[:, :, n_pad:].transpose(0, 2, 1, 3)
    o_a = rms_norm(o_a, gdn_out_norm) * jax.nn.silu(za.astype(f32)).reshape(B, L, GDN_HEADS, GDN_DV)
    o_a = o_a.reshape(B, L, GDN_WIDTH).astype(dt)

    ext_b = jnp.concatenate([buf_mlstm.astype(dt), jnp.concatenate([qb, kb], axis=-1)], axis=1)
    new_buf_mlstm = ext_b[:, L:]
    qk_b = jax.nn.silu(causal_conv(ext_b, conv_mlstm))
    qb, kb = jnp.split(qk_b, [MLSTM_HEADS * MLSTM_DK], axis=-1)
    qb = heads(qb, MLSTM_HEADS)
    kb = heads(kb, MLSTM_HEADS) * (MLSTM_DK ** -0.5)
    vb = heads(vb, MLSTM_HEADS)
    logi = (ib.astype(f32) + mlstm_i_bias.astype(f32)).transpose(0, 2, 1)
    logf = jax.nn.log_sigmoid(fb.astype(f32) + mlstm_f_bias.astype(f32)).transpose(0, 2, 1)
    o_b, C_new, n_new, m_new = mlstm_chunked(pad_front(qb), pad_front(kb), pad_front(vb), pad_front(logi, -jnp.inf),
                                             pad_front(logf), C0.astype(f32), n0.astype(f32), m0.astype(f32), chunk)
    o_b = o_b[:, :, n_pad:].transpose(0, 2, 1, 3)
    o_b = rms_norm(o_b, mlstm_out_norm).reshape(B, L, MLSTM_WIDTH) * jax.nn.sigmoid(ob.astype(f32))
    o_b = o_b.astype(dt)

    y_a = jnp.einsum('blc,cd->bld', o_a, w_branch[:GDN_WIDTH])
    y_b = jnp.einsum('blc,cd->bld', o_b, w_branch[GDN_WIDTH:])
    merged = jax.nn.sigmoid(ga) * y_a + jax.nn.sigmoid(gb) * y_b
    out = jnp.einsum('bld,de->ble', merged, w_out)
    states = (S_new.astype(S0.dtype), new_buf_gdn.astype(buf_gdn.dtype), C_new.astype(C0.dtype),
              n_new.astype(n0.dtype), m_new.astype(m0.dtype), new_buf_mlstm.astype(buf_mlstm.dtype))
    return out, states


def peer(xn, w_query, sub_keys, expert_u, expert_v):
    B, L, D = xn.shape
    T = B * L
    n_pad = (-T) % PEER_BLOCK
    xt = jnp.pad(xn.reshape(T, D), ((0, n_pad), (0, 0))).reshape(-1, PEER_BLOCK, D)

    def block(xb):
        qh = jnp.einsum('td,de->te', xb, w_query).reshape(PEER_BLOCK, PEER_HEADS, 2, KEY_HALF)
        s = jnp.einsum('thpd,hpnd->thpn', qh, sub_keys).astype(jnp.float32)
        sv, si = lax.top_k(s, PEER_TOPK)
        cand = (sv[:, :, 0, :, None] + sv[:, :, 1, None, :]).reshape(PEER_BLOCK, PEER_HEADS, PEER_TOPK * PEER_TOPK)
        cidx = (si[:, :, 0, :, None] * N_KEYS + si[:, :, 1, None, :]).reshape(PEER_BLOCK, PEER_HEADS, PEER_TOPK * PEER_TOPK)
        top_s, pos = lax.top_k(cand, PEER_TOPK)
        eidx = jnp.take_along_axis(cidx, pos, axis=-1)
        gate = jax.nn.softmax(top_s, axis=-1).astype(xb.dtype)
        act = jax.nn.gelu(jnp.einsum('thkd,td->thk', expert_u[eidx], xb), approximate=False)
        return jnp.einsum('thk,thkd->td', gate * act, expert_v[eidx])

    y = lax.map(block, xt).reshape(-1, D)[:T]
    return y.reshape(B, L, D)


def run_trunk(h, st_gdn_S, st_gdn_conv, st_C, st_n, st_m, st_mconv, norm_mix, w_in, conv_gdn, gdn_a_log, gdn_dt_bias,
              gdn_out_norm, conv_mlstm, mlstm_i_bias, mlstm_f_bias, mlstm_out_norm, w_branch, w_out, norm_ffn,
              peer_w_query, peer_sub_keys, peer_u, peer_v, norm_final, n_pad, chunk, n_drop):
    new = [[] for _ in range(6)]
    for l in range(DEPTH):
        xn = rms_norm(h, norm_mix[l])
        out, states = token_mixers(xn, st_gdn_S[l], st_gdn_conv[l], st_C[l], st_n[l], st_m[l], st_mconv[l],
                                   w_in[l], conv_gdn[l], gdn_a_log[l], gdn_dt_bias[l], gdn_out_norm[l],
                                   conv_mlstm[l], mlstm_i_bias[l], mlstm_f_bias[l], mlstm_out_norm[l],
                                   w_branch[l], w_out[l], n_pad, chunk)
        for lst, s in zip(new, states):
            lst.append(s)
        h = h + out
        if l == DEPTH - 1:
            h = h[:, n_drop:]
        h = h + peer(rms_norm(h, norm_ffn[l]), peer_w_query[l], peer_sub_keys[l], peer_u[l], peer_v[l])
    y = rms_norm(h, norm_final)
    return y, [jnp.stack(lst) for lst in new]


def setup_inputs(seed: int = 0) -> dict:
    key = jax.random.key(seed)
    ks = jax.random.split(key, 32)
    f32 = jnp.float32

    def nrm(k, shape, s):
        return jax.random.normal(k, shape, f32) * s

    dt_init = jnp.exp(jax.random.uniform(ks[13], (DEPTH, GDN_HEADS), f32, math.log(1e-3), math.log(1e-1)))
    return {
        'x_prompt': nrm(ks[0], (BATCH, SEQ, D_MODEL), 1.0),
        'x_sample': nrm(ks[1], (DEC_BATCH, DEC_SEQ, D_MODEL), 1.0),
        'state_gdn_S': nrm(ks[2], (DEPTH, DEC_BATCH, GDN_HEADS, GDN_DV, GDN_DK), 0.1),
        'state_gdn_conv': nrm(ks[3], (DEPTH, DEC_BATCH, CONV_W - 1, GDN_CONV_DIM), 1.0),
        'state_mlstm_C': nrm(ks[4], (DEPTH, DEC_BATCH, MLSTM_HEADS, MLSTM_DV, MLSTM_DK), 0.1),
        'state_mlstm_n': nrm(ks[5], (DEPTH, DEC_BATCH, MLSTM_HEADS, MLSTM_DK), 0.3),
        'state_mlstm_m': nrm(ks[6], (DEPTH, DEC_BATCH, MLSTM_HEADS), 1.0),
        'state_mlstm_conv': nrm(ks[7], (DEPTH, DEC_BATCH, CONV_W - 1, MLSTM_CONV_DIM), 1.0),
        'meta_tokens': nrm(ks[8], (N_META, D_MODEL), 1.0),
        'norm_mix': 1.0 + nrm(ks[9], (DEPTH, D_MODEL), 0.02),
        'w_in': nrm(ks[10], (DEPTH, D_MODEL, N_IN), D_MODEL ** -0.5),
        'conv_gdn': nrm(ks[11], (DEPTH, CONV_W, GDN_CONV_DIM), CONV_W ** -0.5),
        'gdn_a_log': jnp.log(jax.random.uniform(ks[12], (DEPTH, GDN_HEADS), f32, 1.0, 16.0)),
        'gdn_dt_bias': dt_init + jnp.log(-jnp.expm1(-dt_init)),
        'gdn_out_norm': 1.0 + nrm(ks[14], (DEPTH, GDN_DV), 0.02),
        'conv_mlstm': nrm(ks[15], (DEPTH, CONV_W, MLSTM_CONV_DIM), CONV_W ** -0.5),
        'mlstm_i_bias': nrm(ks[16], (DEPTH, MLSTM_HEADS), 0.1),
        'mlstm_f_bias': jnp.linspace(3.0, 6.0, MLSTM_HEADS, dtype=f32)[None] + nrm(ks[17], (DEPTH, MLSTM_HEADS), 0.1),
        'mlstm_out_norm': 1.0 + nrm(ks[18], (DEPTH, MLSTM_HEADS, MLSTM_DV), 0.02),
        'w_branch': nrm(ks[19], (DEPTH, GDN_WIDTH + MLSTM_WIDTH, D_MODEL), GDN_WIDTH ** -0.5),
        'w_out': nrm(ks[20], (DEPTH, D_MODEL, D_MODEL), D_MODEL ** -0.5),
        'norm_ffn': 1.0 + nrm(ks[21], (DEPTH, D_MODEL), 0.02),
        'peer_w_query': nrm(ks[22], (DEPTH, D_MODEL, PEER_HEADS * PEER_DQ), D_MODEL ** -0.5),
        'peer_sub_keys': nrm(ks[23], (DEPTH, PEER_HEADS, 2, N_KEYS, KEY_HALF), KEY_HALF ** -0.5),
        'peer_u': nrm(ks[24], (DEPTH, N_EXPERTS, D_MODEL), D_MODEL ** -0.5),
        'peer_v': nrm(ks[25], (DEPTH, N_EXPERTS, D_MODEL), 0.1),
        'norm_final': 1.0 + nrm(ks[26], (D_MODEL,), 0.02),
    }


def reference(x_prompt, x_sample, state_gdn_S, state_gdn_conv, state_mlstm_C, state_mlstm_n, state_mlstm_m,
              state_mlstm_conv, meta_tokens, norm_mix, w_in, conv_gdn, gdn_a_log, gdn_dt_bias, gdn_out_norm,
              conv_mlstm, mlstm_i_bias, mlstm_f_bias, mlstm_out_norm, w_branch, w_out, norm_ffn, peer_w_query,
              peer_sub_keys, peer_u, peer_v, norm_final):
    weights = (norm_mix, w_in, conv_gdn, gdn_a_log, gdn_dt_bias, gdn_out_norm, conv_mlstm, mlstm_i_bias,
               mlstm_f_bias, mlstm_out_norm, w_branch, w_out, norm_ffn, peer_w_query, peer_sub_keys, peer_u,
               peer_v, norm_final)
    f32 = jnp.float32
    B = x_prompt.shape[0]
    dt = x_prompt.dtype
    sdt = state_gdn_S.dtype
    h_p = jnp.concatenate([jnp.broadcast_to(meta_tokens.astype(dt)[None], (B, N_META, D_MODEL)), x_prompt], axis=1)
    y_prompt, p_states = run_trunk(
        h_p,
        jnp.zeros((DEPTH, B, GDN_HEADS, GDN_DV, GDN_DK), sdt),
        jnp.zeros((DEPTH, B, CONV_W - 1, GDN_CONV_DIM), state_gdn_conv.dtype),
        jnp.zeros((DEPTH, B, MLSTM_HEADS, MLSTM_DV, MLSTM_DK), state_mlstm_C.dtype),
        jnp.zeros((DEPTH, B, MLSTM_HEADS, MLSTM_DK), state_mlstm_n.dtype),
        jnp.zeros((DEPTH, B, MLSTM_HEADS), state_mlstm_m.dtype),
        jnp.zeros((DEPTH, B, CONV_W - 1, MLSTM_CONV_DIM), state_mlstm_conv.dtype),
        *weights, n_pad=CHUNK - N_META, chunk=CHUNK, n_drop=N_META)
    y_sample, s_states = run_trunk(
        x_sample, state_gdn_S, state_gdn_conv, state_mlstm_C, state_mlstm_n, state_mlstm_m, state_mlstm_conv,
        *weights, n_pad=0, chunk=x_sample.shape[1], n_drop=0)
    p_S, p_conv, p_C, p_n, p_m, p_mconv = p_states
    s_S, s_conv, s_C, s_n, s_m, s_mconv = s_states
    return (y_prompt, y_sample, p_S, p_conv, p_C, p_n, p_m, p_mconv, s_S, s_conv, s_C, s_n, s_m, s_mconv)
```

```python
import functools
import math

import jax
import jax.numpy as jnp
from jax import lax
from jax.experimental import pallas as pl
from jax.experimental.pallas import tpu as pltpu

F32 = jnp.float32
BF16 = jnp.bfloat16
HIGHEST = lax.Precision.HIGHEST

RMS_EPS = 1e-6
D_MODEL = 1024
N_META = 16
CHUNK = 64
HEADS = 4
DK = 128
DV = 128
PEER_HEADS = 8
N_KEYS = 128
PEER_TOPK = 16
LANES = 128
TAIL = 8
CONV_W = 4
VMEM_LIMIT = 56 * 1024 * 1024

C_QKVA, C_ZA, C_QKB, C_VB, C_OB, C_GA, C_GB, C_SMALL = 0, 1536, 2048, 3072, 3584, 4096, 5120, 6144
N_PROJ = 6272
PROJ_TN = 896
L_A, L_B, L_I, L_F = 0, 4, 8, 12


def _rms(x, gain):
    return x * lax.rsqrt(jnp.mean(x * x, axis=-1, keepdims=True) + RMS_EPS) * gain


def _mm(a, b):
    return jnp.dot(a.astype(BF16), b.astype(BF16), preferred_element_type=F32)


def _mm_nt(a, b):
    return lax.dot_general(a.astype(BF16), b.astype(BF16), (((1,), (1,)), ((), ())), preferred_element_type=F32)


def _mm_tn(a, b):
    return lax.dot_general(a.astype(BF16), b.astype(BF16), (((0,), (0,)), ((), ())), preferred_element_type=F32)


def _mm_hi(a, b):
    return jnp.dot(a, b, preferred_element_type=F32, precision=HIGHEST)


def _norm_proj_kernel(x_ref, g_ref, w_ref, o_ref, xn_ref):
    @pl.when(pl.program_id(1) == 0)
    def _():
        xn_ref[...] = _rms(x_ref[...], g_ref[...]).astype(BF16)

    o_ref[...] = jnp.dot(xn_ref[...], w_ref[...], preferred_element_type=F32)


def _norm_proj(x2d, gain, w_bf16, tm):
    rows = x2d.shape[0]
    return pl.pallas_call(
        _norm_proj_kernel,
        grid=(rows // tm, N_PROJ // PROJ_TN),
        in_specs=[pl.BlockSpec((tm, D_MODEL), lambda i, j: (i, 0)),
                  pl.BlockSpec((1, D_MODEL), lambda i, j: (0, 0)),
                  pl.BlockSpec((D_MODEL, PROJ_TN), lambda i, j: (0, j))],
        out_specs=pl.BlockSpec((tm, PROJ_TN), lambda i, j: (i, j)),
        out_shape=jax.ShapeDtypeStruct((rows, N_PROJ), F32),
        scratch_shapes=[pltpu.VMEM((tm, D_MODEL), BF16)],
        compiler_params=pltpu.CompilerParams(dimension_semantics=("arbitrary", "arbitrary"),
                                             vmem_limit_bytes=VMEM_LIMIT),
        name="norm_proj",
    )(x2d, gain, w_bf16)


def _causal_conv(x, p8, w):
    xe = jnp.concatenate([p8, x], axis=0)
    y = x * w[CONV_W - 1:CONV_W]
    for k in range(1, CONV_W):
        y = y + pltpu.roll(xe, k, 0)[TAIL:] * w[CONV_W - 1 - k:CONV_W - k]
    return y, xe[x.shape[0]:]


def _inv_unit_lower(low, rowi, coli):
    c = low.shape[0]
    eye = (rowi == coli).astype(F32)
    x = eye - jnp.where(((rowi & 1) == 1) & (coli == rowi - 1), low, 0.0)
    shift = 1
    while (1 << shift) < c:
        br = rowi >> shift
        bc = coli >> shift
        cm = jnp.where(((br & 1) == 1) & (bc == br - 1), low, 0.0)
        x = x - _mm_hi(_mm_hi(x, cm), x)
        shift += 1
    return x


def _mixer_kernel(C, nc, npad,
                  qkva_ref, za_ref, qkb_ref, vb_ref, og_ref, sm_ref,
                  ta0_ref, tb0_ref, s0_ref, c0_ref, n0_ref, m0_ref,
                  cwa_ref, cwb_ref, prm_ref, gnorm_ref, mnorm_ref,
                  oa_ref, ob_ref, so_ref, tao_ref, co_ref, no_ref, mo_ref, tbo_ref,
                  s_sc, c_sc, n_sc, m_sc, pa_sc, pb_sc):
    ci = pl.program_id(1)

    @pl.when(ci == 0)
    def _init():
        s_sc[...] = s0_ref[0]
        c_sc[...] = c0_ref[0]
        n_sc[...] = n0_ref[0]
        m_sc[...] = m0_ref[0]
        pa_sc[...] = ta0_ref[0]
        pb_sc[...] = tb0_ref[0]

    rowi = lax.broadcasted_iota(jnp.int32, (C, C), 0)
    coli = lax.broadcasted_iota(jnp.int32, (C, C), 1)
    incl = rowi >= coli
    strict = rowi > coli
    lane = lax.broadcasted_iota(jnp.int32, (C, LANES), 1)
    rown = lax.broadcasted_iota(jnp.int32, (C, LANES), 0)
    neg_inf = -jnp.inf

    qkva, tail_a = _causal_conv(qkva_ref[0], pa_sc[...], cwa_ref[...])
    qkva = jax.nn.silu(qkva)
    qkb, tail_b = _causal_conv(qkb_ref[0], pb_sc[...], cwb_ref[...])
    qkb = jax.nn.silu(qkb)
    pa_sc[...] = tail_a
    pb_sc[...] = tail_b

    sm = sm_ref[0]
    g_full = -jnp.exp(prm_ref[1:2]) * jax.nn.softplus(sm + prm_ref[0:1])
    beta_full = jax.nn.sigmoid(sm)
    logi_full = sm + prm_ref[2:3]
    logf_full = jax.nn.log_sigmoid(sm + prm_ref[3:4])
    if npad:
        real = rown >= npad
        g_full = jnp.where(real, g_full, 0.0)
        beta_full = jnp.where(real, beta_full, 0.0)
        logf_full = jnp.where(real, logf_full, 0.0)
    cs_in = jnp.where(lane < L_A + HEADS, g_full,
                      jnp.where((lane >= L_F) & (lane < L_F + HEADS), logf_full, 0.0))
    cums = _mm_hi(incl.astype(F32), cs_in)
    colform = jnp.where((lane >= L_I) & (lane < L_I + HEADS), logi_full, cums)
    rowform = colform.T

    for h in range(HEADS):
        hs = slice(h * DK, (h + 1) * DK)
        q = qkva[:, h * DK:(h + 1) * DK]
        k = qkva[:, HEADS * DK + h * DK:HEADS * DK + (h + 1) * DK]
        v = qkva[:, 2 * HEADS * DK + h * DV:2 * HEADS * DK + (h + 1) * DV]
        q = q * lax.rsqrt(jnp.sum(q * q, axis=-1, keepdims=True) + RMS_EPS) * (DK ** -0.5)
        k = k * lax.rsqrt(jnp.sum(k * k, axis=-1, keepdims=True) + RMS_EPS)
        gcol = cums[:, L_A + h:L_A + h + 1]
        grow = rowform[L_A + h:L_A + h + 1, :]
        beta = beta_full[:, L_B + h:L_B + h + 1]
        decay = jnp.exp(jnp.where(incl, gcol - grow, neg_inf))
        kk = _mm_nt(k, k)
        low = jnp.where(strict, beta * decay * kk, 0.0)
        xinv = _inv_unit_lower(low, rowi, coli)
        eg = jnp.exp(gcol)
        u = _mm_hi(xinv, beta * v)
        w = _mm_hi(xinv, (beta * eg) * k)
        qk = _mm_nt(q, k) * decay
        glast = cums[C - 1:C, L_A + h:L_A + h + 1]
        q_dec = q * eg
        k_end = k * jnp.exp(glast - gcol)
        g_end = jnp.exp(glast)
        s = s_sc[h]
        delta = u - _mm_nt(w, s)
        o = _mm_nt(q_dec, s) + _mm(qk, delta)
        s_sc[h] = g_end * s + _mm_tn(delta, k_end)
        o = _rms(o, gnorm_ref[...]) * jax.nn.silu(za_ref[0, :, hs])
        oa_ref[0, :, hs] = o.astype(BF16)

        qb = qkb[:, h * DK:(h + 1) * DK]
        kb = qkb[:, HEADS * DK + h * DK:HEADS * DK + (h + 1) * DK] * (DK ** -0.5)
        vb = vb_ref[0, :, hs]
        fcol = cums[:, L_F + h:L_F + h + 1]
        frow = rowform[L_F + h:L_F + h + 1, :]
        licol = logi_full[:, L_I + h:L_I + h + 1]
        lirow = rowform[L_I + h:L_I + h + 1, :]
        if npad:
            licol = jnp.where(rown[:, 0:1] >= npad, licol, neg_inf)
            lirow = jnp.where(coli[0:1, :] >= npad, lirow, neg_inf)
        dmat = jnp.where(incl, fcol - frow + lirow, neg_inf)
        dmax = jnp.max(dmat, axis=1, keepdims=True)
        qkm = _mm_nt(qb, kb)
        ms = m_sc[h:h + 1, 0:1]
        m_inter = fcol + ms
        m_t = jnp.maximum(m_inter, dmax)
        w_intra = jnp.exp(dmat - m_t) * qkm
        w_inter = jnp.exp(m_inter - m_t)
        cs = c_sc[h]
        ns = n_sc[h:h + 1, :]
        num = w_inter * _mm_nt(qb, cs) + _mm(w_intra, vb)
        den = w_inter * jnp.sum(qb * ns, axis=1, keepdims=True) + jnp.sum(w_intra, axis=1, keepdims=True)
        hh = num / jnp.maximum(jnp.abs(den), jnp.exp(-m_t))
        f_end = cums[C - 1:C, L_F + h:L_F + h + 1]
        e_col = f_end - fcol + licol
        e_max = jnp.max(e_col, axis=0, keepdims=True)
        m_new = jnp.maximum(f_end + ms, e_max)
        dec = jnp.exp(f_end + ms - m_new)
        wk = jnp.exp(e_col - m_new)
        c_sc[h] = dec * cs + _mm_tn(wk * vb, kb)
        n_sc[h:h + 1, :] = dec * ns + jnp.sum(wk * kb, axis=0, keepdims=True)
        m_sc[h:h + 1, :] = jnp.broadcast_to(m_new, (1, LANES))
        hh = _rms(hh, mnorm_ref[h:h + 1, :]) * jax.nn.sigmoid(og_ref[0, :, hs])
        ob_ref[0, :, hs] = hh.astype(BF16)

    @pl.when(ci == nc - 1)
    def _fin():
        so_ref[0] = s_sc[...]
        co_ref[0] = c_sc[...]
        no_ref[0] = n_sc[...]
        mo_ref[0] = m_sc[...]
        tao_ref[0] = pa_sc[...]
        tbo_ref[0] = pb_sc[...]


def _mixers(proj3, init, consts, C, npad, shared_init):
    B, L, _ = proj3.shape
    nc = L // C
    cwa, cwb, prm, gnorm, mnorm = consts
    wa = cwa.shape[1]
    wb = cwb.shape[1]

    def col(width, start):
        return pl.BlockSpec((1, C, width), lambda b, c, _s=start // width: (b, c, _s))

    def st(shape):
        nd = len(shape)
        if shared_init:
            return pl.BlockSpec((1,) + shape, lambda b, c, _n=nd: (0,) * (_n + 1))
        return pl.BlockSpec((1,) + shape, lambda b, c, _n=nd: (b,) + (0,) * _n)

    def full(a):
        return pl.BlockSpec(a.shape, lambda b, c, _n=a.ndim: (0,) * _n)

    def outst(shape):
        return pl.BlockSpec((1,) + shape, lambda b, c, _n=len(shape): (b,) + (0,) * _n)

    in_specs = [col(wa, C_QKVA), col(512, C_ZA), col(wb, C_QKB), col(512, C_VB), col(512, C_OB), col(LANES, C_SMALL),
                st((TAIL, wa)), st((TAIL, wb)), st((HEADS, DV, DK)), st((HEADS, DV, DK)), st((HEADS, DK)), st((HEADS, LANES)),
                full(cwa), full(cwb), full(prm), full(gnorm), full(mnorm)]
    out_specs = [pl.BlockSpec((1, C, HEADS * DV), lambda b, c: (b, c, 0)),
                 pl.BlockSpec((1, C, HEADS * DV), lambda b, c: (b, c, 0)),
                 outst((HEADS, DV, DK)), outst((TAIL, wa)), outst((HEADS, DV, DK)), outst((HEADS, DK)),
                 outst((HEADS, LANES)), outst((TAIL, wb))]
    out_shape = [jax.ShapeDtypeStruct((B, L, HEADS * DV), BF16), jax.ShapeDtypeStruct((B, L, HEADS * DV), BF16),
                 jax.ShapeDtypeStruct((B, HEADS, DV, DK), F32), jax.ShapeDtypeStruct((B, TAIL, wa), F32),
                 jax.ShapeDtypeStruct((B, HEADS, DV, DK), F32), jax.ShapeDtypeStruct((B, HEADS, DK), F32),
                 jax.ShapeDtypeStruct((B, HEADS, LANES), F32), jax.ShapeDtypeStruct((B, TAIL, wb), F32)]
    scratch = [pltpu.VMEM((HEADS, DV, DK), F32), pltpu.VMEM((HEADS, DV, DK), F32), pltpu.VMEM((HEADS, DK), F32),
               pltpu.VMEM((HEADS, LANES), F32), pltpu.VMEM((TAIL, wa), F32), pltpu.VMEM((TAIL, wb), F32)]
    tail_a, tail_b, s0, c0, n0, m0 = init
    return pl.pallas_call(
        functools.partial(_mixer_kernel, C, nc, npad),
        grid=(B, nc),
        in_specs=in_specs, out_specs=out_specs, out_shape=out_shape, scratch_shapes=scratch,
        compiler_params=pltpu.CompilerParams(dimension_semantics=("arbitrary", "arbitrary"),
                                             vmem_limit_bytes=VMEM_LIMIT),
        name="mixers",
    )(proj3, proj3, proj3, proj3, proj3, proj3, tail_a, tail_b, s0, c0, n0, m0, cwa, cwb, prm, gnorm, mnorm)


def _out_proj_kernel(oa_ref, ob_ref, ga_ref, gb_ref, h_ref, wba_ref, wbb_ref, wo_ref, o_ref):
    ya = jnp.dot(oa_ref[...], wba_ref[...], preferred_element_type=F32)
    yb = jnp.dot(ob_ref[...], wbb_ref[...], preferred_element_type=F32)
    merged = jax.nn.sigmoid(ga_ref[...]) * ya + jax.nn.sigmoid(gb_ref[...]) * yb
    o_ref[...] = h_ref[...] + jnp.dot(merged.astype(BF16), wo_ref[...], preferred_element_type=F32)


def _out_proj(oa, ob, proj, h2d, wba, wbb, wo, tm):
    rows = h2d.shape[0]
    w_half = HEADS * DV

    def rowblk(width, cidx=0):
        return pl.BlockSpec((tm, width), lambda i, _c=cidx: (i, _c))

    def full(a):
        return pl.BlockSpec(a.shape, lambda i: (0, 0))

    return pl.pallas_call(
        _out_proj_kernel,
        grid=(rows // tm,),
        in_specs=[rowblk(w_half), rowblk(w_half), rowblk(D_MODEL, C_GA // D_MODEL), rowblk(D_MODEL, C_GB // D_MODEL),
                  rowblk(D_MODEL), full(wba), full(wbb), full(wo)],
        out_specs=rowblk(D_MODEL),
        out_shape=jax.ShapeDtypeStruct((rows, D_MODEL), F32),
        compiler_params=pltpu.CompilerParams(dimension_semantics=("arbitrary",), vmem_limit_bytes=VMEM_LIMIT),
        name="out_proj",
    )(oa, ob, proj, proj, h2d, wba, wbb, wo)


N_RANK = PEER_TOPK + 1
_STAIR = [(k, N_RANK // (k + 1)) for k in range(N_RANK)]
_N_CAND = sum(n for _, n in _STAIR)
_CAND_ROWS = -(-_N_CAND // 8) * 8


def _peer_kernel(TB, NI,
                 h_ref, gf_ref, wq_ref, keys_ref, u_ref, vt_ref, gfin_ref, o_ref,
                 xn_sc, q_sc, s1_sc, e1_sc, r_sc, e0_sc, top_sc, cand_sc, acc_sc):
    e = pl.program_id(1)
    ne = pl.num_programs(1)
    neg_inf = -jnp.inf

    @pl.when(e == 0)
    def _prep():
        xn = _rms(h_ref[...], gf_ref[...]).astype(BF16)
        xn_sc[...] = xn
        q = jnp.dot(xn, wq_ref[...], preferred_element_type=F32)
        for j in range(2 * PEER_HEADS):
            q_sc[j] = q[:, j * LANES:(j + 1) * LANES].astype(BF16)
        cand_sc[...] = jnp.full((_CAND_ROWS, TB), neg_inf, F32)

        def head(hd, carry):
            s_pair = []
            for p in range(2):
                s_t = lax.dot_general(keys_ref[2 * hd + p], q_sc[2 * hd + p], (((1,), (1,)), ((), ())),
                                      preferred_element_type=F32)
                s_pair.append(s_t)
                x = s_t
                for kk in range(N_RANK):
                    mx = jnp.max(x, axis=0, keepdims=True)
                    top_sc[p, kk:kk + 1, :] = mx
                    x = jnp.where(x == mx, neg_inf, x)
            off = 0
            for kk, n in _STAIR:
                cand_sc[off:off + n, :] = top_sc[0, kk:kk + 1, :] + top_sc[1, 0:n, :]
                off += n
            x = cand_sc[...]
            a0 = top_sc[0, 0:1, :]
            b0 = top_sc[1, 0:1, :]
            top = a0 + b0
            z = jnp.zeros((1, TB), F32)
            tau = top
            for kk in range(PEER_TOPK):
                tau = jnp.max(x, axis=0, keepdims=True)
                z = z + jnp.exp(tau - top)
                x = jnp.where(x == tau, neg_inf, x)
            tau = 0.5 * (tau + jnp.max(x, axis=0, keepdims=True))
            s1_sc[hd] = s_pair[1]
            e1_sc[hd] = jnp.exp(s_pair[1] - b0)
            r_sc[hd] = tau - s_pair[0]
            e0_sc[hd] = jnp.exp(s_pair[0] - a0) / z
            return carry

        lax.fori_loop(0, PEER_HEADS, head, 0)
        acc_sc[...] = jnp.zeros_like(acc_sc)

    act = lax.dot_general(u_ref[...], xn_sc[...], (((1,), (1,)), ((), ())), preferred_element_type=F32)
    act = 0.5 * act * (1.0 + lax.erf(act * (2.0 ** -0.5)))
    parts = []
    for ii in range(NI):
        i_glob = e * NI + ii
        wgt = jnp.zeros((N_KEYS, TB), F32)
        for hd in range(PEER_HEADS):
            thr = r_sc[hd, pl.ds(i_glob, 1), :]
            e0 = e0_sc[hd, pl.ds(i_glob, 1), :]
            wgt = wgt + jnp.where(s1_sc[hd] >= thr, e1_sc[hd], 0.0) * e0
        parts.append((act[ii * N_KEYS:(ii + 1) * N_KEYS] * wgt).astype(BF16))
    a_t = jnp.concatenate(parts, axis=0) if NI > 1 else parts[0]
    acc_sc[...] += jnp.dot(vt_ref[...], a_t, preferred_element_type=F32)

    @pl.when(e == ne - 1)
    def _fin():
        o_ref[...] = _rms(h_ref[...] + acc_sc[...].T, gfin_ref[...])


def _peer(h2d, gf, wq, keys, u_bf, vt_bf, gfin, tb, ni):
    rows = h2d.shape[0]
    n_exp = u_bf.shape[0]
    eb = ni * N_KEYS
    return pl.pallas_call(
        functools.partial(_peer_kernel, tb, ni),
        grid=(rows // tb, n_exp // eb),
        in_specs=[pl.BlockSpec((tb, D_MODEL), lambda i, e: (i, 0)),
                  pl.BlockSpec((1, D_MODEL), lambda i, e: (0, 0)),
                  pl.BlockSpec(wq.shape, lambda i, e: (0, 0)),
                  pl.BlockSpec(keys.shape, lambda i, e: (0, 0, 0)),
                  pl.BlockSpec((eb, D_MODEL), lambda i, e: (e, 0)),
                  pl.BlockSpec((D_MODEL, eb), lambda i, e: (0, e)),
                  pl.BlockSpec((1, D_MODEL), lambda i, e: (0, 0))],
        out_specs=pl.BlockSpec((tb, D_MODEL), lambda i, e: (i, 0)),
        out_shape=jax.ShapeDtypeStruct((rows, D_MODEL), F32),
        scratch_shapes=[pltpu.VMEM((tb, D_MODEL), BF16),
                        pltpu.VMEM((2 * PEER_HEADS, tb, LANES), BF16),
                        pltpu.VMEM((PEER_HEADS, N_KEYS, tb), F32),
                        pltpu.VMEM((PEER_HEADS, N_KEYS, tb), F32),
                        pltpu.VMEM((PEER_HEADS, N_KEYS, tb), F32),
                        pltpu.VMEM((PEER_HEADS, N_KEYS, tb), F32),
                        pltpu.VMEM((2, -(-N_RANK // 8) * 8, tb), F32),
                        pltpu.VMEM((_CAND_ROWS, tb), F32),
                        pltpu.VMEM((D_MODEL, tb), F32)],
        compiler_params=pltpu.CompilerParams(dimension_semantics=("arbitrary", "arbitrary"),
                                             vmem_limit_bytes=VMEM_LIMIT),
        name="peer",
    )(h2d, gf, wq, keys, u_bf, vt_bf, gfin)


def _tile(rows, cap):
    t = cap
    while t > 8 and rows % t:
        t //= 2
    return t if rows % t == 0 else rows


def _stream(x3, init, shared_init, chunk, npad, wts, run_ffn):
    (norm_mix, w_proj, consts, wba, wbb, wo, norm_ffn, wq, keys, u_bf, vt_bf, norm_final) = wts
    B, L, _ = x3.shape
    x2 = x3.reshape(B * L, D_MODEL)
    tm = _tile(B * L, 1024)
    proj = _norm_proj(x2, norm_mix, w_proj, tm)
    outs = _mixers(proj.reshape(B, L, N_PROJ), init, consts, chunk, npad, shared_init)
    oa, ob, s_new, tail_a, c_new, n_new, m_new, tail_b = outs
    states = (tail_a, tail_b, s_new, c_new, n_new, m_new)
    if not run_ffn:
        return None, states
    h2 = _out_proj(oa.reshape(B * L, -1), ob.reshape(B * L, -1), proj, x2, wba, wbb, wo, _tile(B * L, 512))
    y = _peer(h2, norm_ffn, wq, keys, u_bf, vt_bf, norm_final, _tile(B * L, 512), 4)
    return y.reshape(B, L, D_MODEL), states


def kernel(x_prompt, x_sample, state_gdn_S, state_gdn_conv, state_mlstm_C, state_mlstm_n, state_mlstm_m, state_mlstm_conv, meta_tokens, norm_mix, w_in, conv_gdn, gdn_a_log, gdn_dt_bias, gdn_out_norm, conv_mlstm, mlstm_i_bias, mlstm_f_bias, mlstm_out_norm, w_branch, w_out, norm_ffn, peer_w_query, peer_sub_keys, peer_u, peer_v, norm_final):
    assert w_in.shape[0] == 1, "one trunk layer"
    w = w_in[0]
    w_proj = jnp.concatenate(
        [w[:, 0:1536], w[:, 1544:2056], w[:, 2056:3080], w[:, 3080:3592], w[:, 3600:4112], w[:, 4112:6160],
         w[:, 1536:1544], w[:, 3592:3600], jnp.zeros((D_MODEL, N_PROJ - C_SMALL - 4 * HEADS), w.dtype)],
        axis=1).astype(BF16)

    def lane_row(vec, start):
        return jnp.zeros((LANES,), F32).at[start:start + HEADS].set(vec.astype(F32))

    prm = jnp.zeros((8, LANES), F32)
    prm = prm.at[0].set(lane_row(gdn_dt_bias[0], L_A)).at[1].set(lane_row(gdn_a_log[0], L_A))
    prm = prm.at[2].set(lane_row(mlstm_i_bias[0], L_I)).at[3].set(lane_row(mlstm_f_bias[0], L_F))
    consts = (conv_gdn[0], conv_mlstm[0], prm, gdn_out_norm[0].reshape(1, DV), mlstm_out_norm[0])
    wb = w_branch[0].astype(BF16)
    wts = (norm_mix[0].reshape(1, D_MODEL), w_proj, consts, wb[:HEADS * DV], wb[HEADS * DV:], w_out[0].astype(BF16),
           norm_ffn[0].reshape(1, D_MODEL), peer_w_query[0].astype(BF16),
           peer_sub_keys[0].reshape(2 * PEER_HEADS, N_KEYS, LANES).astype(BF16),
           peer_u[0].astype(BF16), peer_v[0].astype(BF16).T, norm_final.reshape(1, D_MODEL))

    def pad_tail(t):
        return jnp.pad(t, ((0, 0), (TAIL - (CONV_W - 1), 0), (0, 0)))

    def lanes(m):
        return jnp.broadcast_to(m[..., None], m.shape + (LANES,))

    wa = conv_gdn.shape[-1]
    wbw = conv_mlstm.shape[-1]
    zero_init = (jnp.zeros((1, TAIL, wa), F32), jnp.zeros((1, TAIL, wbw), F32),
                 jnp.zeros((1, HEADS, DV, DK), F32), jnp.zeros((1, HEADS, DV, DK), F32),
                 jnp.zeros((1, HEADS, DK), F32), jnp.zeros((1, HEADS, LANES), F32))
    x_meta = jnp.concatenate([jnp.zeros((CHUNK - N_META, D_MODEL), F32), meta_tokens.astype(F32)], axis=0)[None]
    _, meta_states = _stream(x_meta, zero_init, True, CHUNK, CHUNK - N_META, wts, False)
    y_p, st_p = _stream(x_prompt, meta_states, True, CHUNK, 0, wts, True)
    samp_init = (pad_tail(state_gdn_conv[0]), pad_tail(state_mlstm_conv[0]), state_gdn_S[0], state_mlstm_C[0],
                 state_mlstm_n[0], lanes(state_mlstm_m[0]))
    y_s, st_s = _stream(x_sample, samp_init, False, x_sample.shape[1], 0, wts, True)

    def pack(st):
        tail_a, tail_b, s_new, c_new, n_new, m_new = st
        return (s_new[None], tail_a[:, TAIL - (CONV_W - 1):][None], c_new[None], n_new[None], m_new[:, :, 0][None],
                tail_b[:, TAIL - (CONV_W - 1):][None])

    return (y_p, y_s) + pack(st_p) + pack(st_s)
```

```python
import functools

import jax
import jax.numpy as jnp
from jax import lax
from jax.experimental import pallas as pl
from jax.experimental.pallas import tpu as pltpu

F32 = jnp.float32
BF16 = jnp.bfloat16
HIGHEST = lax.Precision.HIGHEST

RMS_EPS = 1e-6
D_MODEL = 1024
N_META = 16
CHUNK = 64
HEADS = 4
DK = 128
DV = 128
PEER_HEADS = 8
N_KEYS = 128
PEER_TOPK = 16
LANES = 128
TAIL = 8
CONV_W = 4
VMEM_LIMIT = 56 * 1024 * 1024

C_QKVA, C_ZA, C_QKB, C_VB, C_OB, C_GA, C_GB, C_SMALL = 0, 1536, 2048, 3072, 3584, 4096, 5120, 6144
N_PROJ = 6272
PROJ_TN = 896
L_A, L_B, L_I, L_F = 0, 4, 8, 12


def _rms(x, gain):
    return x * lax.rsqrt(jnp.mean(x * x, axis=-1, keepdims=True) + RMS_EPS) * gain


def _mm(a, b):
    return jnp.dot(a.astype(BF16), b.astype(BF16), preferred_element_type=F32)


def _mm_nt(a, b):
    return lax.dot_general(a.astype(BF16), b.astype(BF16), (((1,), (1,)), ((), ())), preferred_element_type=F32)


def _mm_tn(a, b):
    return lax.dot_general(a.astype(BF16), b.astype(BF16), (((0,), (0,)), ((), ())), preferred_element_type=F32)


def _mm_hi(a, b):
    return jnp.dot(a, b, preferred_element_type=F32, precision=HIGHEST)


def _norm_proj_kernel(x_ref, g_ref, w_ref, o_ref, xn_ref):
    @pl.when(pl.program_id(1) == 0)
    def _():
        xn_ref[...] = _rms(x_ref[...], g_ref[...]).astype(BF16)

    o_ref[...] = jnp.dot(xn_ref[...], w_ref[...], preferred_element_type=F32)


def _norm_proj(x2d, gain, w_bf16, tm):
    rows = x2d.shape[0]
    return pl.pallas_call(
        _norm_proj_kernel,
        grid=(rows // tm, N_PROJ // PROJ_TN),
        in_specs=[pl.BlockSpec((tm, D_MODEL), lambda i, j: (i, 0)),
                  pl.BlockSpec((1, D_MODEL), lambda i, j: (0, 0)),
                  pl.BlockSpec((D_MODEL, PROJ_TN), lambda i, j: (0, j))],
        out_specs=pl.BlockSpec((tm, PROJ_TN), lambda i, j: (i, j)),
        out_shape=jax.ShapeDtypeStruct((rows, N_PROJ), F32),
        scratch_shapes=[pltpu.VMEM((tm, D_MODEL), BF16)],
        compiler_params=pltpu.CompilerParams(dimension_semantics=("arbitrary", "arbitrary"),
                                             vmem_limit_bytes=VMEM_LIMIT),
        name="norm_proj",
    )(x2d, gain, w_bf16)


def _causal_conv(x, p8, w):
    xe = jnp.concatenate([p8, x], axis=0)
    y = x * w[CONV_W - 1:CONV_W]
    for k in range(1, CONV_W):
        y = y + pltpu.roll(xe, k, 0)[TAIL:] * w[CONV_W - 1 - k:CONV_W - k]
    return y, xe[x.shape[0]:]


def _inv_unit_lower(low, rowi, coli, blk):
    eye = (rowi == coli).astype(F32)
    x = eye - jnp.where(((rowi & 1) == 1) & (coli == rowi - 1), low, 0.0)
    shift = 1
    while (1 << shift) < blk:
        br = rowi >> shift
        bc = coli >> shift
        cm = jnp.where(((br & 1) == 1) & (bc == br - 1), low, 0.0)
        x = x - _mm(_mm(x, cm), x)
        shift += 1
    return x


def _mixer_kernel(C, nc, npad, NB, NBI,
                  qkva_ref, za_ref, qkb_ref, vb_ref, og_ref, sm_ref,
                  ta0_ref, tb0_ref, s0_ref, c0_ref, n0_ref, m0_ref,
                  cwa_ref, cwb_ref, prm_ref, gnorm_ref, mnorm_ref,
                  oa_ref, ob_ref, so_ref, tao_ref, co_ref, no_ref, mo_ref, tbo_ref,
                  s_sc, c_sc, n_sc, m_sc, pa_sc, pb_sc):
    ci = pl.program_id(1)
    NBLK = NB * HEADS
    R = NBLK * C
    T = NB * C
    log2c = C.bit_length() - 1
    blocks = [(n, h) for n in range(NB) for h in range(HEADS)]

    def rows(b):
        return slice(b * C, (b + 1) * C)

    def stack(pieces):
        return jnp.concatenate(pieces, axis=0)

    def wide(x):
        return jnp.concatenate([x] * (R // LANES), axis=1)

    @pl.when(ci == 0)
    def _init():
        for n in range(NB):
            ni = min(n, NBI - 1)
            s_sc[n] = s0_ref[ni]
            c_sc[n] = c0_ref[ni]
            n_sc[n] = n0_ref[ni]
            pa_sc[n] = ta0_ref[ni]
            pb_sc[n] = tb0_ref[ni]
            for h in range(HEADS):
                m_sc[rows(n * HEADS + h), :] = jnp.broadcast_to(m0_ref[ni, h:h + 1, :], (C, LANES))

    rowi = lax.broadcasted_iota(jnp.int32, (R, R), 0)
    coli = lax.broadcasted_iota(jnp.int32, (R, R), 1)
    same = (rowi >> log2c) == (coli >> log2c)
    incl = same & (rowi >= coli)
    strict = same & (rowi > coli)
    neg_inf = -jnp.inf

    qkva, qkb = [], []
    for n in range(NB):
        ya, tail_a = _causal_conv(qkva_ref[n], pa_sc[n], cwa_ref[...])
        yb, tail_b = _causal_conv(qkb_ref[n], pb_sc[n], cwb_ref[...])
        pa_sc[n] = tail_a
        pb_sc[n] = tail_b
        qkva.append(jax.nn.silu(ya))
        qkb.append(jax.nn.silu(yb))
    hd = HEADS * DK
    q_all = stack([qkva[n][:, h * DK:(h + 1) * DK] for n, h in blocks])
    k_all = stack([qkva[n][:, hd + h * DK:hd + (h + 1) * DK] for n, h in blocks])
    v_all = stack([qkva[n][:, 2 * hd + h * DV:2 * hd + (h + 1) * DV] for n, h in blocks])
    qb_all = stack([qkb[n][:, h * DK:(h + 1) * DK] for n, h in blocks])
    kb_all = stack([qkb[n][:, hd + h * DK:hd + (h + 1) * DK] for n, h in blocks]) * (DK ** -0.5)
    vb_all = stack([vb_ref[n, :, h * DV:(h + 1) * DV] for n, h in blocks])

    sm = stack([sm_ref[n] for n in range(NB)])
    lane = lax.broadcasted_iota(jnp.int32, (T, LANES), 1)
    g_full = -jnp.exp(prm_ref[1:2]) * jax.nn.softplus(sm + prm_ref[0:1])
    beta_full = jax.nn.sigmoid(sm)
    logi_full = sm + prm_ref[2:3]
    logf_full = jax.nn.log_sigmoid(sm + prm_ref[3:4])
    if npad:
        real = (lax.broadcasted_iota(jnp.int32, (T, LANES), 0) & (C - 1)) >= npad
        g_full = jnp.where(real, g_full, 0.0)
        beta_full = jnp.where(real, beta_full, 0.0)
        logf_full = jnp.where(real, logf_full, 0.0)
    cs_in = jnp.where(lane < L_A + HEADS, g_full,
                      jnp.where((lane >= L_F) & (lane < L_F + HEADS), logf_full, 0.0))
    ti = lax.broadcasted_iota(jnp.int32, (T, T), 0)
    tj = lax.broadcasted_iota(jnp.int32, (T, T), 1)
    tril = (((ti >> log2c) == (tj >> log2c)) & (ti >= tj)).astype(F32)
    cums = _mm_hi(tril, cs_in)
    gates = jnp.where((lane >= L_B) & (lane < L_B + HEADS), beta_full,
                      jnp.where((lane >= L_I) & (lane < L_I + HEADS), logi_full, cums))
    last = stack([jnp.broadcast_to(cums[n * C + C - 1:n * C + C, :], (C, LANES)) for n in range(NB)])

    def col(src, base):
        return stack([jnp.broadcast_to(src[n * C:(n + 1) * C, base + h:base + h + 1], (C, LANES)) for n, h in blocks])

    g_c, beta_c, li_c, f_c = col(gates, L_A), col(gates, L_B), col(gates, L_I), col(gates, L_F)
    glast_c, fend_c = col(last, L_A), col(last, L_F)
    lane_r = lax.broadcasted_iota(jnp.int32, (R, LANES), 1)
    rowform = jnp.where(lane_r == 0, g_c, jnp.where(lane_r == 1, li_c, f_c)).T
    g_r, li_r, f_r = rowform[0:1, :], rowform[1:2, :], rowform[2:3, :]
    if npad:
        li_c = jnp.where((lax.broadcasted_iota(jnp.int32, (R, LANES), 0) & (C - 1)) >= npad, li_c, neg_inf)
        li_r = jnp.where((coli[0:1, :] & (C - 1)) >= npad, li_r, neg_inf)

    q_all = q_all * lax.rsqrt(jnp.sum(q_all * q_all, axis=-1, keepdims=True) + RMS_EPS) * (DK ** -0.5)
    k_all = k_all * lax.rsqrt(jnp.sum(k_all * k_all, axis=-1, keepdims=True) + RMS_EPS)
    decay = jnp.exp(jnp.where(incl, wide(g_c) - g_r, neg_inf))
    low = jnp.where(strict, wide(beta_c) * decay * _mm_nt(k_all, k_all), 0.0)
    xinv = _inv_unit_lower(low, rowi, coli, C)
    eg = jnp.exp(g_c)
    u_all = _mm(xinv, beta_c * v_all)
    w_all = _mm(xinv, (beta_c * eg) * k_all)
    qk = _mm_nt(q_all, k_all) * decay
    q_dec = q_all * eg
    k_end = k_all * jnp.exp(glast_c - g_c)
    g_end = jnp.exp(glast_c)
    deltas, inter = [], []
    for b, (n, h) in enumerate(blocks):
        s = s_sc[n, h]
        delta = u_all[rows(b)] - _mm_nt(w_all[rows(b)], s)
        inter.append(_mm_nt(q_dec[rows(b)], s))
        s_sc[n, h] = g_end[b * C:b * C + 1, :] * s + _mm_tn(delta, k_end[rows(b)])
        deltas.append(delta)
    o_all = stack(inter) + _mm(qk, stack(deltas))
    za_all = stack([za_ref[n, :, h * DV:(h + 1) * DV] for n, h in blocks])
    o_all = (_rms(o_all, gnorm_ref[...]) * jax.nn.silu(za_all)).astype(BF16)
    for b, (n, h) in enumerate(blocks):
        oa_ref[n, :, h * DV:(h + 1) * DV] = o_all[rows(b)]

    dmat = jnp.where(incl, wide(f_c) - f_r + li_r, neg_inf)
    dmax = jnp.broadcast_to(jnp.max(dmat, axis=1, keepdims=True), (R, LANES))
    qkm = _mm_nt(qb_all, kb_all)
    ms = m_sc[...]
    m_inter = f_c + ms
    m_t = jnp.maximum(m_inter, dmax)
    w_intra = jnp.exp(dmat - wide(m_t)) * qkm
    w_inter = jnp.exp(m_inter - m_t)
    e_col = fend_c - f_c + li_c
    e_max = jnp.max(e_col.reshape(NBLK, C, LANES), axis=1, keepdims=True)
    e_max = jnp.broadcast_to(e_max, (NBLK, C, LANES)).reshape(R, LANES)
    m_new = jnp.maximum(fend_c + ms, e_max)
    dec = jnp.exp(fend_c + ms - m_new)
    wk = jnp.exp(e_col - m_new)
    wkv = wk * vb_all
    wkk = wk * kb_all
    inter, ns_rows = [], []
    for b, (n, h) in enumerate(blocks):
        cs = c_sc[n, h]
        ns = n_sc[n, h:h + 1, :]
        inter.append(_mm_nt(qb_all[rows(b)], cs))
        ns_rows.append(jnp.broadcast_to(ns, (C, LANES)))
        dec_b = dec[b * C:b * C + 1, :]
        c_sc[n, h] = dec_b * cs + _mm_tn(wkv[rows(b)], kb_all[rows(b)])
        n_sc[n, h:h + 1, :] = dec_b * ns + jnp.sum(wkk[rows(b)], axis=0, keepdims=True)
    m_sc[...] = m_new
    num = w_inter * stack(inter) + _mm(w_intra, vb_all)
    den = (w_inter[:, 0:1] * jnp.sum(qb_all * stack(ns_rows), axis=1, keepdims=True)
           + jnp.sum(w_intra, axis=1, keepdims=True))
    hh = num / jnp.maximum(jnp.abs(den), jnp.exp(-m_t[:, 0:1]))
    mnorm = stack([jnp.broadcast_to(mnorm_ref[h:h + 1, :], (C, DV)) for _, h in blocks])
    og_all = stack([og_ref[n, :, h * DV:(h + 1) * DV] for n, h in blocks])
    hh = (_rms(hh, mnorm) * jax.nn.sigmoid(og_all)).astype(BF16)
    for b, (n, h) in enumerate(blocks):
        ob_ref[n, :, h * DV:(h + 1) * DV] = hh[rows(b)]

    @pl.when(ci == nc - 1)
    def _fin():
        so_ref[...] = s_sc[...]
        co_ref[...] = c_sc[...]
        no_ref[...] = n_sc[...]
        tao_ref[...] = pa_sc[...]
        tbo_ref[...] = pb_sc[...]
        for b, (n, h) in enumerate(blocks):
            mo_ref[n, h:h + 1, :] = m_sc[b * C:b * C + 1, :]


def _mixers(proj3, init, consts, C, npad):
    B, L, _ = proj3.shape
    nc = L // C
    assert C & (C - 1) == 0 and (2 * LANES) % (HEADS * C) == 0, "chunk must stack into 2*LANES rows"
    NB = (2 * LANES) // (HEADS * C)
    assert B % NB == 0
    cwa, cwb, prm, gnorm, mnorm = consts
    wa = cwa.shape[1]
    wb = cwb.shape[1]
    tail_a, tail_b, s0, c0, n0, m0 = init
    shared_init = s0.shape[0] == 1 and B > 1
    assert shared_init or s0.shape[0] == B
    NBI = 1 if shared_init else NB
    wh = HEADS * DV

    def col(width, start):
        return pl.BlockSpec((NB, C, width), lambda b, c, _s=start // width: (b, c, _s))

    def st(shape):
        nd = len(shape)
        if shared_init:
            return pl.BlockSpec((1,) + shape, lambda b, c, _n=nd: (0,) * (_n + 1))
        return pl.BlockSpec((NB,) + shape, lambda b, c, _n=nd: (b,) + (0,) * _n)

    def full(a):
        return pl.BlockSpec(a.shape, lambda b, c, _n=a.ndim: (0,) * _n)

    def outst(shape):
        return pl.BlockSpec((NB,) + shape, lambda b, c, _n=len(shape): (b,) + (0,) * _n)

    in_specs = [col(wa, C_QKVA), col(wh, C_ZA), col(wb, C_QKB), col(wh, C_VB), col(wh, C_OB), col(LANES, C_SMALL),
                st((TAIL, wa)), st((TAIL, wb)), st((HEADS, DV, DK)), st((HEADS, DV, DK)), st((HEADS, DK)), st((HEADS, LANES)),
                full(cwa), full(cwb), full(prm), full(gnorm), full(mnorm)]
    out_specs = [pl.BlockSpec((NB, C, wh), lambda b, c: (b, c, 0)),
                 pl.BlockSpec((NB, C, wh), lambda b, c: (b, c, 0)),
                 outst((HEADS, DV, DK)), outst((TAIL, wa)), outst((HEADS, DV, DK)), outst((HEADS, DK)),
                 outst((HEADS, LANES)), outst((TAIL, wb))]
    out_shape = [jax.ShapeDtypeStruct((B, L, wh), BF16), jax.ShapeDtypeStruct((B, L, wh), BF16),
                 jax.ShapeDtypeStruct((B, HEADS, DV, DK), F32), jax.ShapeDtypeStruct((B, TAIL, wa), F32),
                 jax.ShapeDtypeStruct((B, HEADS, DV, DK), F32), jax.ShapeDtypeStruct((B, HEADS, DK), F32),
                 jax.ShapeDtypeStruct((B, HEADS, LANES), F32), jax.ShapeDtypeStruct((B, TAIL, wb), F32)]
    scratch = [pltpu.VMEM((NB, HEADS, DV, DK), F32), pltpu.VMEM((NB, HEADS, DV, DK), F32),
               pltpu.VMEM((NB, HEADS, DK), F32), pltpu.VMEM((NB * HEADS * C, LANES), F32),
               pltpu.VMEM((NB, TAIL, wa), F32), pltpu.VMEM((NB, TAIL, wb), F32)]
    return pl.pallas_call(
        functools.partial(_mixer_kernel, C, nc, npad, NB, NBI),
        grid=(B // NB, nc),
        in_specs=in_specs, out_specs=out_specs, out_shape=out_shape, scratch_shapes=scratch,
        compiler_params=pltpu.CompilerParams(dimension_semantics=("arbitrary", "arbitrary"),
                                             vmem_limit_bytes=VMEM_LIMIT),
        name="mixers",
    )(proj3, proj3, proj3, proj3, proj3, proj3, tail_a, tail_b, s0, c0, n0, m0, cwa, cwb, prm, gnorm, mnorm)


def _out_proj_kernel(oa_ref, ob_ref, ga_ref, gb_ref, h_ref, wba_ref, wbb_ref, wo_ref, o_ref):
    ya = jnp.dot(oa_ref[...], wba_ref[...], preferred_element_type=F32)
    yb = jnp.dot(ob_ref[...], wbb_ref[...], preferred_element_type=F32)
    merged = jax.nn.sigmoid(ga_ref[...]) * ya + jax.nn.sigmoid(gb_ref[...]) * yb
    o_ref[...] = h_ref[...] + jnp.dot(merged.astype(BF16), wo_ref[...], preferred_element_type=F32)


def _out_proj(oa, ob, proj, h2d, wba, wbb, wo, tm):
    rows = h2d.shape[0]
    w_half = HEADS * DV

    def rowblk(width, cidx=0):
        return pl.BlockSpec((tm, width), lambda i, _c=cidx: (i, _c))

    def full(a):
        return pl.BlockSpec(a.shape, lambda i: (0, 0))

    return pl.pallas_call(
        _out_proj_kernel,
        grid=(rows // tm,),
        in_specs=[rowblk(w_half), rowblk(w_half), rowblk(D_MODEL, C_GA // D_MODEL), rowblk(D_MODEL, C_GB // D_MODEL),
                  rowblk(D_MODEL), full(wba), full(wbb), full(wo)],
        out_specs=rowblk(D_MODEL),
        out_shape=jax.ShapeDtypeStruct((rows, D_MODEL), F32),
        compiler_params=pltpu.CompilerParams(dimension_semantics=("arbitrary",), vmem_limit_bytes=VMEM_LIMIT),
        name="out_proj",
    )(oa, ob, proj, proj, h2d, wba, wbb, wo)


N_RANK = PEER_TOPK + 1
_STAIR = [(k, N_RANK // (k + 1)) for k in range(N_RANK)]
_N_CAND = sum(n for _, n in _STAIR)
_CAND_ROWS = -(-_N_CAND // 8) * 8


def _peer_kernel(TB, NI, NBLK,
                 h_ref, gf_ref, wq_ref, keys_ref, u_ref, vt_ref, gfin_ref, o_ref,
                 xn_sc, q_sc, s1_sc, e1_sc, r_sc, e0_sc, top_sc, cand_sc, acc_sc, act0_sc, act1_sc, a0_sc, a1_sc):
    g = pl.program_id(1)
    ng = pl.num_programs(1)
    EB = NI * N_KEYS
    neg_inf = -jnp.inf

    @pl.when(g == 0)
    def _prep():
        for buf in (act0_sc, act1_sc, a0_sc, a1_sc):
            buf[...] = jnp.zeros_like(buf)
        xn = _rms(h_ref[...], gf_ref[...]).astype(BF16)
        xn_sc[...] = xn
        q = jnp.dot(xn, wq_ref[...], preferred_element_type=F32)
        for j in range(2 * PEER_HEADS):
            q_sc[j] = q[:, j * LANES:(j + 1) * LANES].astype(BF16)
        cand_sc[...] = jnp.full((_CAND_ROWS, TB), neg_inf, F32)

        def head(hd, carry):
            s_pair = []
            for p in range(2):
                s_t = lax.dot_general(keys_ref[2 * hd + p], q_sc[2 * hd + p], (((1,), (1,)), ((), ())),
                                      preferred_element_type=F32)
                s_pair.append(s_t)
                x = s_t
                for kk in range(N_RANK):
                    mx = jnp.max(x, axis=0, keepdims=True)
                    top_sc[p, kk:kk + 1, :] = mx
                    x = jnp.where(x == mx, neg_inf, x)
            off = 0
            for kk, n in _STAIR:
                cand_sc[off:off + n, :] = top_sc[0, kk:kk + 1, :] + top_sc[1, 0:n, :]
                off += n
            x = cand_sc[...]
            a0 = top_sc[0, 0:1, :]
            b0 = top_sc[1, 0:1, :]
            top = a0 + b0
            z = jnp.zeros((1, TB), F32)
            tau = top
            for kk in range(PEER_TOPK):
                tau = jnp.max(x, axis=0, keepdims=True)
                z = z + jnp.exp(tau - top)
                x = jnp.where(x == tau, neg_inf, x)
            tau = 0.5 * (tau + jnp.max(x, axis=0, keepdims=True))
            s1_sc[hd] = s_pair[1]
            e1_sc[hd] = jnp.exp(s_pair[1] - b0)
            r_sc[hd] = tau - s_pair[0]
            e0_sc[hd] = jnp.exp(s_pair[0] - a0) / z
            return carry

        lax.fori_loop(0, PEER_HEADS, head, 0)
        acc_sc[...] = jnp.zeros_like(acc_sc)

    act_bufs = (act0_sc, act1_sc)
    a_bufs = (a0_sc, a1_sc)
    for sub in range(2):
        cur, prev = sub, 1 - sub
        blk = jnp.clip(2 * g + sub - 1, 0, NBLK - 1)
        for ii in range(NI):
            i_glob = blk * NI + ii
            rows = slice(ii * N_KEYS, (ii + 1) * N_KEYS)
            act = act_bufs[prev][rows, :]
            act = 0.5 * act * (1.0 + lax.erf(act * (2.0 ** -0.5)))
            wgt = jnp.zeros((N_KEYS, TB), F32)
            for hd in range(PEER_HEADS):
                thr = r_sc[hd, pl.ds(i_glob, 1), :]
                e0 = e0_sc[hd, pl.ds(i_glob, 1), :]
                wgt = wgt + jnp.where(s1_sc[hd] >= thr, e1_sc[hd], 0.0) * e0
            a_bufs[cur][rows, :] = act * wgt
        act_bufs[cur][...] = lax.dot_general(u_ref[sub * EB:(sub + 1) * EB, :], xn_sc[...], (((1,), (1,)), ((), ())),
                                             preferred_element_type=F32)
        acc_sc[...] += jnp.dot(vt_ref[:, sub * EB:(sub + 1) * EB], a_bufs[prev][...].astype(BF16),
                               preferred_element_type=F32)

    @pl.when(g == ng - 1)
    def _fin():
        o_ref[...] = _rms(h_ref[...] + acc_sc[...].T, gfin_ref[...])


def _peer(h2d, gf, wq, keys, u_bf, vt_bf, gfin, tb, ni):
    rows = h2d.shape[0]
    n_exp = u_bf.shape[0]
    eb = ni * N_KEYS
    nblk = n_exp // eb
    last = nblk // 2 - 1
    return pl.pallas_call(
        functools.partial(_peer_kernel, tb, ni, nblk),
        grid=(rows // tb, nblk // 2 + 1),
        in_specs=[pl.BlockSpec((tb, D_MODEL), lambda i, e: (i, 0)),
                  pl.BlockSpec((1, D_MODEL), lambda i, e: (0, 0)),
                  pl.BlockSpec(wq.shape, lambda i, e: (0, 0)),
                  pl.BlockSpec(keys.shape, lambda i, e: (0, 0, 0)),
                  pl.BlockSpec((2 * eb, D_MODEL), lambda i, e: (jnp.minimum(e, last), 0)),
                  pl.BlockSpec((D_MODEL, 2 * eb), lambda i, e: (0, jnp.clip(e - 1, 0, last))),
                  pl.BlockSpec((1, D_MODEL), lambda i, e: (0, 0))],
        out_specs=pl.BlockSpec((tb, D_MODEL), lambda i, e: (i, 0)),
        out_shape=jax.ShapeDtypeStruct((rows, D_MODEL), F32),
        scratch_shapes=[pltpu.VMEM((tb, D_MODEL), BF16),
                        pltpu.VMEM((2 * PEER_HEADS, tb, LANES), BF16),
                        pltpu.VMEM((PEER_HEADS, N_KEYS, tb), F32),
                        pltpu.VMEM((PEER_HEADS, N_KEYS, tb), F32),
                        pltpu.VMEM((PEER_HEADS, N_KEYS, tb), F32),
                        pltpu.VMEM((PEER_HEADS, N_KEYS, tb), F32),
                        pltpu.VMEM((2, -(-N_RANK // 8) * 8, tb), F32),
                        pltpu.VMEM((_CAND_ROWS, tb), F32),
                        pltpu.VMEM((D_MODEL, tb), F32),
                        pltpu.VMEM((eb, tb), F32), pltpu.VMEM((eb, tb), F32),
                        pltpu.VMEM((eb, tb), F32), pltpu.VMEM((eb, tb), F32)],
        compiler_params=pltpu.CompilerParams(dimension_semantics=("arbitrary", "arbitrary"),
                                             vmem_limit_bytes=VMEM_LIMIT),
        name="peer",
    )(h2d, gf, wq, keys, u_bf, vt_bf, gfin)


def _tile(rows, cap):
    t = cap
    while t > 8 and rows % t:
        t //= 2
    return t if rows % t == 0 else rows


def _stream(x3, init, chunk, npad, wts, run_ffn):
    (norm_mix, w_proj, consts, wba, wbb, wo, norm_ffn, wq, keys, u_bf, vt_bf, norm_final) = wts
    B, L, _ = x3.shape
    x2 = x3.reshape(B * L, D_MODEL)
    proj = _norm_proj(x2, norm_mix, w_proj, _tile(B * L, 1024))
    outs = _mixers(proj.reshape(B, L, N_PROJ), init, consts, chunk, npad)
    oa, ob, s_new, tail_a, c_new, n_new, m_new, tail_b = outs
    states = (tail_a, tail_b, s_new, c_new, n_new, m_new)
    if not run_ffn:
        return None, states
    h2 = _out_proj(oa.reshape(B * L, -1), ob.reshape(B * L, -1), proj, x2, wba, wbb, wo, _tile(B * L, 512))
    y = _peer(h2, norm_ffn, wq, keys, u_bf, vt_bf, norm_final, _tile(B * L, 512), 4)
    return y.reshape(B, L, D_MODEL), states


def kernel(x_prompt, x_sample, state_gdn_S, state_gdn_conv, state_mlstm_C, state_mlstm_n, state_mlstm_m, state_mlstm_conv, meta_tokens, norm_mix, w_in, conv_gdn, gdn_a_log, gdn_dt_bias, gdn_out_norm, conv_mlstm, mlstm_i_bias, mlstm_f_bias, mlstm_out_norm, w_branch, w_out, norm_ffn, peer_w_query, peer_sub_keys, peer_u, peer_v, norm_final):
    assert w_in.shape[0] == 1, "one trunk layer"
    w = w_in[0]
    w_proj = jnp.concatenate(
        [w[:, 0:1536], w[:, 1544:2056], w[:, 2056:3080], w[:, 3080:3592], w[:, 3600:4112], w[:, 4112:6160],
         w[:, 1536:1544], w[:, 3592:3600], jnp.zeros((D_MODEL, N_PROJ - C_SMALL - 4 * HEADS), w.dtype)],
        axis=1).astype(BF16)

    def lane_row(vec, start):
        return jnp.zeros((LANES,), F32).at[start:start + HEADS].set(vec.astype(F32))

    prm = jnp.zeros((8, LANES), F32)
    prm = prm.at[0].set(lane_row(gdn_dt_bias[0], L_A)).at[1].set(lane_row(gdn_a_log[0], L_A))
    prm = prm.at[2].set(lane_row(mlstm_i_bias[0], L_I)).at[3].set(lane_row(mlstm_f_bias[0], L_F))
    consts = (conv_gdn[0], conv_mlstm[0], prm, gdn_out_norm[0].reshape(1, DV), mlstm_out_norm[0])
    wb = w_branch[0].astype(BF16)
    wts = (norm_mix[0].reshape(1, D_MODEL), w_proj, consts, wb[:HEADS * DV], wb[HEADS * DV:], w_out[0].astype(BF16),
           norm_ffn[0].reshape(1, D_MODEL), peer_w_query[0].astype(BF16),
           peer_sub_keys[0].reshape(2 * PEER_HEADS, N_KEYS, LANES).astype(BF16),
           peer_u[0].astype(BF16), peer_v[0].astype(BF16).T, norm_final.reshape(1, D_MODEL))

    def pad_tail(t):
        return jnp.pad(t, ((0, 0), (TAIL - (CONV_W - 1), 0), (0, 0)))

    def lanes(m):
        return jnp.broadcast_to(m[..., None], m.shape + (LANES,))

    wa = conv_gdn.shape[-1]
    wbw = conv_mlstm.shape[-1]
    zero_init = (jnp.zeros((1, TAIL, wa), F32), jnp.zeros((1, TAIL, wbw), F32),
                 jnp.zeros((1, HEADS, DV, DK), F32), jnp.zeros((1, HEADS, DV, DK), F32),
                 jnp.zeros((1, HEADS, DK), F32), jnp.zeros((1, HEADS, LANES), F32))
    x_meta = jnp.concatenate([jnp.zeros((CHUNK - N_META, D_MODEL), F32), meta_tokens.astype(F32)], axis=0)[None]
    _, meta_states = _stream(x_meta, zero_init, CHUNK, CHUNK - N_META, wts, False)
    y_p, st_p = _stream(x_prompt, meta_states, CHUNK, 0, wts, True)
    samp_init = (pad_tail(state_gdn_conv[0]), pad_tail(state_mlstm_conv[0]), state_gdn_S[0], state_mlstm_C[0],
                 state_mlstm_n[0], lanes(state_mlstm_m[0]))
    y_s, st_s = _stream(x_sample, samp_init, x_sample.shape[1], 0, wts, True)

    def pack(st):
        tail_a, tail_b, s_new, c_new, n_new, m_new = st
        return (s_new[None], tail_a[:, TAIL - (CONV_W - 1):][None], c_new[None], n_new[None], m_new[:, :, 0][None],
                tail_b[:, TAIL - (CONV_W - 1):][None])

    return (y_p, y_s) + pack(st_p) + pack(st_s)
```

```python
import functools

import jax
import jax.numpy as jnp
from jax import lax
from jax.experimental import pallas as pl
from jax.experimental.pallas import tpu as pltpu

F32 = jnp.float32
BF16 = jnp.bfloat16
HIGHEST = lax.Precision.HIGHEST

RMS_EPS = 1e-6
D_MODEL = 1024
N_META = 16
CHUNK = 64
HEADS = 4
DK = 128
DV = 128
PEER_HEADS = 8
N_KEYS = 128
PEER_TOPK = 16
LANES = 128
TAIL = 8
CONV_W = 4
VMEM_LIMIT = 56 * 1024 * 1024

C_QKVA, C_ZA, C_QKB, C_VB, C_OB, C_GA, C_GB, C_SMALL = 0, 1536, 2048, 3072, 3584, 4096, 5120, 6144
N_PROJ = 6272
PROJ_TN = 896
L_A, L_B, L_I, L_F = 0, 4, 8, 12


def _rms(x, gain):
    return x * lax.rsqrt(jnp.mean(x * x, axis=-1, keepdims=True) + RMS_EPS) * gain


def _mm(a, b):
    return jnp.dot(a.astype(BF16), b.astype(BF16), preferred_element_type=F32)


def _mm_nt(a, b):
    return lax.dot_general(a.astype(BF16), b.astype(BF16), (((1,), (1,)), ((), ())), preferred_element_type=F32)


def _mm_tn(a, b):
    return lax.dot_general(a.astype(BF16), b.astype(BF16), (((0,), (0,)), ((), ())), preferred_element_type=F32)


def _mm_hi(a, b):
    return jnp.dot(a, b, preferred_element_type=F32, precision=HIGHEST)


def _norm_proj_kernel(x_ref, g_ref, w_ref, o_ref, xn_ref):
    @pl.when(pl.program_id(1) == 0)
    def _():
        xn_ref[...] = _rms(x_ref[...], g_ref[...]).astype(BF16)

    o_ref[...] = jnp.dot(xn_ref[...], w_ref[...], preferred_element_type=F32)


def _norm_proj(x2d, gain, w_bf16, tm):
    rows = x2d.shape[0]
    return pl.pallas_call(
        _norm_proj_kernel,
        grid=(rows // tm, N_PROJ // PROJ_TN),
        in_specs=[pl.BlockSpec((tm, D_MODEL), lambda i, j: (i, 0)),
                  pl.BlockSpec((1, D_MODEL), lambda i, j: (0, 0)),
                  pl.BlockSpec((D_MODEL, PROJ_TN), lambda i, j: (0, j))],
        out_specs=pl.BlockSpec((tm, PROJ_TN), lambda i, j: (i, j)),
        out_shape=jax.ShapeDtypeStruct((rows, N_PROJ), F32),
        scratch_shapes=[pltpu.VMEM((tm, D_MODEL), BF16)],
        compiler_params=pltpu.CompilerParams(dimension_semantics=("arbitrary", "arbitrary"),
                                             vmem_limit_bytes=VMEM_LIMIT),
        name="norm_proj",
    )(x2d, gain, w_bf16)


def _causal_conv(x, p8, w):
    xe = jnp.concatenate([p8, x], axis=0)
    y = x * w[CONV_W - 1:CONV_W]
    for k in range(1, CONV_W):
        y = y + pltpu.roll(xe, k, 0)[TAIL:] * w[CONV_W - 1 - k:CONV_W - k]
    return y, xe[x.shape[0]:]


def _inv_unit_lower(low, rowi, coli, blk):
    eye = (rowi == coli).astype(F32)
    x = eye - jnp.where(((rowi & 1) == 1) & (coli == rowi - 1), low, 0.0)
    shift = 1
    while (1 << shift) < blk:
        br = rowi >> shift
        bc = coli >> shift
        cm = jnp.where(((br & 1) == 1) & (bc == br - 1), low, 0.0)
        x = x - _mm(_mm(x, cm), x)
        shift += 1
    return x


def _mixer_kernel(C, nc, npad, NB, NBI,
                  qkva_ref, za_ref, qkb_ref, vb_ref, og_ref, sm_ref,
                  ta0_ref, tb0_ref, s0_ref, c0_ref, n0_ref, m0_ref,
                  cwa_ref, cwb_ref, prm_ref, gnorm_ref, mnorm_ref,
                  oa_ref, ob_ref, so_ref, tao_ref, co_ref, no_ref, mo_ref, tbo_ref,
                  s_sc, c_sc, n_sc, m_sc, pa_sc, pb_sc):
    ci = pl.program_id(1)
    NBLK = NB * HEADS
    R = NBLK * C
    T = NB * C
    log2c = C.bit_length() - 1
    blocks = [(n, h) for n in range(NB) for h in range(HEADS)]

    def rows(b):
        return slice(b * C, (b + 1) * C)

    def stack(pieces):
        return jnp.concatenate(pieces, axis=0)

    def wide(x):
        return jnp.concatenate([x] * (R // LANES), axis=1)

    @pl.when(ci == 0)
    def _init():
        for n in range(NB):
            ni = min(n, NBI - 1)
            s_sc[n] = s0_ref[ni]
            c_sc[n] = c0_ref[ni]
            n_sc[n] = n0_ref[ni]
            pa_sc[n] = ta0_ref[ni]
            pb_sc[n] = tb0_ref[ni]
            for h in range(HEADS):
                m_sc[rows(n * HEADS + h), :] = jnp.broadcast_to(m0_ref[ni, h:h + 1, :], (C, LANES))

    rowi = lax.broadcasted_iota(jnp.int32, (R, R), 0)
    coli = lax.broadcasted_iota(jnp.int32, (R, R), 1)
    same = (rowi >> log2c) == (coli >> log2c)
    incl = same & (rowi >= coli)
    strict = same & (rowi > coli)
    neg_inf = -jnp.inf

    qkva, qkb = [], []
    for n in range(NB):
        ya, tail_a = _causal_conv(qkva_ref[n], pa_sc[n], cwa_ref[...])
        yb, tail_b = _causal_conv(qkb_ref[n], pb_sc[n], cwb_ref[...])
        pa_sc[n] = tail_a
        pb_sc[n] = tail_b
        qkva.append(jax.nn.silu(ya))
        qkb.append(jax.nn.silu(yb))
    hd = HEADS * DK
    q_all = stack([qkva[n][:, h * DK:(h + 1) * DK] for n, h in blocks])
    k_all = stack([qkva[n][:, hd + h * DK:hd + (h + 1) * DK] for n, h in blocks])
    v_all = stack([qkva[n][:, 2 * hd + h * DV:2 * hd + (h + 1) * DV] for n, h in blocks])
    qb_all = stack([qkb[n][:, h * DK:(h + 1) * DK] for n, h in blocks])
    kb_all = stack([qkb[n][:, hd + h * DK:hd + (h + 1) * DK] for n, h in blocks]) * (DK ** -0.5)
    vb_all = stack([vb_ref[n, :, h * DV:(h + 1) * DV] for n, h in blocks])

    sm = stack([sm_ref[n] for n in range(NB)])
    lane = lax.broadcasted_iota(jnp.int32, (T, LANES), 1)
    g_full = -jnp.exp(prm_ref[1:2]) * jax.nn.softplus(sm + prm_ref[0:1])
    beta_full = jax.nn.sigmoid(sm)
    logi_full = sm + prm_ref[2:3]
    logf_full = jax.nn.log_sigmoid(sm + prm_ref[3:4])
    if npad:
        real = (lax.broadcasted_iota(jnp.int32, (T, LANES), 0) & (C - 1)) >= npad
        g_full = jnp.where(real, g_full, 0.0)
        beta_full = jnp.where(real, beta_full, 0.0)
        logf_full = jnp.where(real, logf_full, 0.0)
    cs_in = jnp.where(lane < L_A + HEADS, g_full,
                      jnp.where((lane >= L_F) & (lane < L_F + HEADS), logf_full, 0.0))
    ti = lax.broadcasted_iota(jnp.int32, (T, T), 0)
    tj = lax.broadcasted_iota(jnp.int32, (T, T), 1)
    tril = (((ti >> log2c) == (tj >> log2c)) & (ti >= tj)).astype(F32)
    cums = _mm_hi(tril, cs_in)
    gates = jnp.where((lane >= L_B) & (lane < L_B + HEADS), beta_full,
                      jnp.where((lane >= L_I) & (lane < L_I + HEADS), logi_full, cums))
    last = stack([jnp.broadcast_to(cums[n * C + C - 1:n * C + C, :], (C, LANES)) for n in range(NB)])

    def col(src, base):
        return stack([jnp.broadcast_to(src[n * C:(n + 1) * C, base + h:base + h + 1], (C, LANES)) for n, h in blocks])

    g_c, beta_c, li_c, f_c = col(gates, L_A), col(gates, L_B), col(gates, L_I), col(gates, L_F)
    glast_c, fend_c = col(last, L_A), col(last, L_F)
    lane_r = lax.broadcasted_iota(jnp.int32, (R, LANES), 1)
    rowform = jnp.where(lane_r == 0, g_c, jnp.where(lane_r == 1, li_c, f_c)).T
    g_r, li_r, f_r = rowform[0:1, :], rowform[1:2, :], rowform[2:3, :]
    if npad:
        li_c = jnp.where((lax.broadcasted_iota(jnp.int32, (R, LANES), 0) & (C - 1)) >= npad, li_c, neg_inf)
        li_r = jnp.where((coli[0:1, :] & (C - 1)) >= npad, li_r, neg_inf)

    q_all = q_all * lax.rsqrt(jnp.sum(q_all * q_all, axis=-1, keepdims=True) + RMS_EPS) * (DK ** -0.5)
    k_all = k_all * lax.rsqrt(jnp.sum(k_all * k_all, axis=-1, keepdims=True) + RMS_EPS)
    decay = jnp.exp(jnp.where(incl, wide(g_c) - g_r, neg_inf))
    low = jnp.where(strict, wide(beta_c) * decay * _mm_nt(k_all, k_all), 0.0)
    xinv = _inv_unit_lower(low, rowi, coli, C)
    eg = jnp.exp(g_c)
    u_all = _mm(xinv, beta_c * v_all)
    w_all = _mm(xinv, (beta_c * eg) * k_all)
    qk = _mm_nt(q_all, k_all) * decay
    q_dec = q_all * eg
    k_end = k_all * jnp.exp(glast_c - g_c)
    g_end = jnp.exp(glast_c)
    deltas, inter = [], []
    for b, (n, h) in enumerate(blocks):
        s = s_sc[n, h]
        delta = u_all[rows(b)] - _mm_nt(w_all[rows(b)], s)
        inter.append(_mm_nt(q_dec[rows(b)], s))
        s_sc[n, h] = g_end[b * C:b * C + 1, :] * s + _mm_tn(delta, k_end[rows(b)])
        deltas.append(delta)
    o_all = stack(inter) + _mm(qk, stack(deltas))
    za_all = stack([za_ref[n, :, h * DV:(h + 1) * DV] for n, h in blocks])
    o_all = (_rms(o_all, gnorm_ref[...]) * jax.nn.silu(za_all)).astype(BF16)
    for b, (n, h) in enumerate(blocks):
        oa_ref[n, :, h * DV:(h + 1) * DV] = o_all[rows(b)]

    dmat = jnp.where(incl, wide(f_c) - f_r + li_r, neg_inf)
    dmax = jnp.broadcast_to(jnp.max(dmat, axis=1, keepdims=True), (R, LANES))
    qkm = _mm_nt(qb_all, kb_all)
    ms = m_sc[...]
    m_inter = f_c + ms
    m_t = jnp.maximum(m_inter, dmax)
    w_intra = jnp.exp(dmat - wide(m_t)) * qkm
    w_inter = jnp.exp(m_inter - m_t)
    e_col = fend_c - f_c + li_c
    e_max = jnp.max(e_col.reshape(NBLK, C, LANES), axis=1, keepdims=True)
    e_max = jnp.broadcast_to(e_max, (NBLK, C, LANES)).reshape(R, LANES)
    m_new = jnp.maximum(fend_c + ms, e_max)
    dec = jnp.exp(fend_c + ms - m_new)
    wk = jnp.exp(e_col - m_new)
    wkv = wk * vb_all
    wkk = wk * kb_all
    inter, ns_rows = [], []
    for b, (n, h) in enumerate(blocks):
        cs = c_sc[n, h]
        ns = n_sc[n, h:h + 1, :]
        inter.append(_mm_nt(qb_all[rows(b)], cs))
        ns_rows.append(jnp.broadcast_to(ns, (C, LANES)))
        dec_b = dec[b * C:b * C + 1, :]
        c_sc[n, h] = dec_b * cs + _mm_tn(wkv[rows(b)], kb_all[rows(b)])
        n_sc[n, h:h + 1, :] = dec_b * ns + jnp.sum(wkk[rows(b)], axis=0, keepdims=True)
    m_sc[...] = m_new
    num = w_inter * stack(inter) + _mm(w_intra, vb_all)
    den = (w_inter[:, 0:1] * jnp.sum(qb_all * stack(ns_rows), axis=1, keepdims=True)
           + jnp.sum(w_intra, axis=1, keepdims=True))
    hh = num / jnp.maximum(jnp.abs(den), jnp.exp(-m_t[:, 0:1]))
    mnorm = stack([jnp.broadcast_to(mnorm_ref[h:h + 1, :], (C, DV)) for _, h in blocks])
    og_all = stack([og_ref[n, :, h * DV:(h + 1) * DV] for n, h in blocks])
    hh = (_rms(hh, mnorm) * jax.nn.sigmoid(og_all)).astype(BF16)
    for b, (n, h) in enumerate(blocks):
        ob_ref[n, :, h * DV:(h + 1) * DV] = hh[rows(b)]

    @pl.when(ci == nc - 1)
    def _fin():
        so_ref[...] = s_sc[...]
        co_ref[...] = c_sc[...]
        no_ref[...] = n_sc[...]
        tao_ref[...] = pa_sc[...]
        tbo_ref[...] = pb_sc[...]
        for b, (n, h) in enumerate(blocks):
            mo_ref[n, h:h + 1, :] = m_sc[b * C:b * C + 1, :]


def _mixers(proj3, init, consts, C, npad):
    B, L, _ = proj3.shape
    nc = L // C
    assert C & (C - 1) == 0 and (2 * LANES) % (HEADS * C) == 0, "chunk must stack into 2*LANES rows"
    NB = (2 * LANES) // (HEADS * C)
    assert B % NB == 0
    cwa, cwb, prm, gnorm, mnorm = consts
    wa = cwa.shape[1]
    wb = cwb.shape[1]
    tail_a, tail_b, s0, c0, n0, m0 = init
    shared_init = s0.shape[0] == 1 and B > 1
    assert shared_init or s0.shape[0] == B
    NBI = 1 if shared_init else NB
    wh = HEADS * DV

    def col(width, start):
        return pl.BlockSpec((NB, C, width), lambda b, c, _s=start // width: (b, c, _s))

    def st(shape):
        nd = len(shape)
        if shared_init:
            return pl.BlockSpec((1,) + shape, lambda b, c, _n=nd: (0,) * (_n + 1))
        return pl.BlockSpec((NB,) + shape, lambda b, c, _n=nd: (b,) + (0,) * _n)

    def full(a):
        return pl.BlockSpec(a.shape, lambda b, c, _n=a.ndim: (0,) * _n)

    def outst(shape):
        return pl.BlockSpec((NB,) + shape, lambda b, c, _n=len(shape): (b,) + (0,) * _n)

    in_specs = [col(wa, C_QKVA), col(wh, C_ZA), col(wb, C_QKB), col(wh, C_VB), col(wh, C_OB), col(LANES, C_SMALL),
                st((TAIL, wa)), st((TAIL, wb)), st((HEADS, DV, DK)), st((HEADS, DV, DK)), st((HEADS, DK)), st((HEADS, LANES)),
                full(cwa), full(cwb), full(prm), full(gnorm), full(mnorm)]
    out_specs = [pl.BlockSpec((NB, C, wh), lambda b, c: (b, c, 0)),
                 pl.BlockSpec((NB, C, wh), lambda b, c: (b, c, 0)),
                 outst((HEADS, DV, DK)), outst((TAIL, wa)), outst((HEADS, DV, DK)), outst((HEADS, DK)),
                 outst((HEADS, LANES)), outst((TAIL, wb))]
    out_shape = [jax.ShapeDtypeStruct((B, L, wh), BF16), jax.ShapeDtypeStruct((B, L, wh), BF16),
                 jax.ShapeDtypeStruct((B, HEADS, DV, DK), F32), jax.ShapeDtypeStruct((B, TAIL, wa), F32),
                 jax.ShapeDtypeStruct((B, HEADS, DV, DK), F32), jax.ShapeDtypeStruct((B, HEADS, DK), F32),
                 jax.ShapeDtypeStruct((B, HEADS, LANES), F32), jax.ShapeDtypeStruct((B, TAIL, wb), F32)]
    scratch = [pltpu.VMEM((NB, HEADS, DV, DK), F32), pltpu.VMEM((NB, HEADS, DV, DK), F32),
               pltpu.VMEM((NB, HEADS, DK), F32), pltpu.VMEM((NB * HEADS * C, LANES), F32),
               pltpu.VMEM((NB, TAIL, wa), F32), pltpu.VMEM((NB, TAIL, wb), F32)]
    return pl.pallas_call(
        functools.partial(_mixer_kernel, C, nc, npad, NB, NBI),
        grid=(B // NB, nc),
        in_specs=in_specs, out_specs=out_specs, out_shape=out_shape, scratch_shapes=scratch,
        compiler_params=pltpu.CompilerParams(dimension_semantics=("arbitrary", "arbitrary"),
                                             vmem_limit_bytes=VMEM_LIMIT),
        name="mixers",
    )(proj3, proj3, proj3, proj3, proj3, proj3, tail_a, tail_b, s0, c0, n0, m0, cwa, cwb, prm, gnorm, mnorm)


def _out_proj_kernel(oa_ref, ob_ref, ga_ref, gb_ref, h_ref, wba_ref, wbb_ref, wo_ref, o_ref):
    ya = jnp.dot(oa_ref[...], wba_ref[...], preferred_element_type=F32)
    yb = jnp.dot(ob_ref[...], wbb_ref[...], preferred_element_type=F32)
    merged = jax.nn.sigmoid(ga_ref[...]) * ya + jax.nn.sigmoid(gb_ref[...]) * yb
    o_ref[...] = h_ref[...] + jnp.dot(merged.astype(BF16), wo_ref[...], preferred_element_type=F32)


def _out_proj(oa, ob, proj, h2d, wba, wbb, wo, tm):
    rows = h2d.shape[0]
    w_half = HEADS * DV

    def rowblk(width, cidx=0):
        return pl.BlockSpec((tm, width), lambda i, _c=cidx: (i, _c))

    def full(a):
        return pl.BlockSpec(a.shape, lambda i: (0, 0))

    return pl.pallas_call(
        _out_proj_kernel,
        grid=(rows // tm,),
        in_specs=[rowblk(w_half), rowblk(w_half), rowblk(D_MODEL, C_GA // D_MODEL), rowblk(D_MODEL, C_GB // D_MODEL),
                  rowblk(D_MODEL), full(wba), full(wbb), full(wo)],
        out_specs=rowblk(D_MODEL),
        out_shape=jax.ShapeDtypeStruct((rows, D_MODEL), F32),
        compiler_params=pltpu.CompilerParams(dimension_semantics=("arbitrary",), vmem_limit_bytes=VMEM_LIMIT),
        name="out_proj",
    )(oa, ob, proj, proj, h2d, wba, wbb, wo)


N_RANK = PEER_TOPK + 1
_STAIR = [(k, N_RANK // (k + 1)) for k in range(N_RANK)]
_N_CAND = sum(n for _, n in _STAIR)
_CAND_ROWS = -(-_N_CAND // 8) * 8


def _peer_kernel(TB, NI, NBLK,
                 h_ref, gf_ref, wq_ref, keys_ref, u_ref, vta_ref, vtb_ref, gfin_ref, o_ref,
                 xn_sc, q_sc, r1_sc, e1_sc, cnt_sc, e0_sc, top_sc, cand_sc, acc_sc, act0_sc, act1_sc):
    g = pl.program_id(1)
    ng = pl.num_programs(1)
    EB = NI * N_KEYS
    neg_inf = -jnp.inf

    @pl.when(g == 0)
    def _prep():
        act0_sc[...] = jnp.zeros_like(act0_sc)
        act1_sc[...] = jnp.zeros_like(act1_sc)
        xn = _rms(h_ref[...], gf_ref[...]).astype(BF16)
        xn_sc[...] = xn
        q = jnp.dot(xn, wq_ref[...], preferred_element_type=F32)
        for j in range(2 * PEER_HEADS):
            q_sc[j] = q[:, j * LANES:(j + 1) * LANES].astype(BF16)
        cand_sc[...] = jnp.full((_CAND_ROWS, TB), neg_inf, F32)

        def head(hd, carry):
            s_pair = []
            rank1 = jnp.full((N_KEYS, TB), float(N_KEYS - 1), F32)
            for p in range(2):
                s_t = lax.dot_general(keys_ref[2 * hd + p], q_sc[2 * hd + p], (((1,), (1,)), ((), ())),
                                      preferred_element_type=F32)
                s_pair.append(s_t)
                x = s_t
                for kk in range(N_RANK):
                    mx = jnp.max(x, axis=0, keepdims=True)
                    top_sc[p, kk:kk + 1, :] = mx
                    hit = x == mx
                    if p == 1:
                        rank1 = jnp.where(hit, float(kk), rank1)
                    x = jnp.where(hit, neg_inf, x)
            off = 0
            for kk, n in _STAIR:
                cand_sc[off:off + n, :] = top_sc[0, kk:kk + 1, :] + top_sc[1, 0:n, :]
                off += n
            x = cand_sc[...]
            a0 = top_sc[0, 0:1, :]
            b0 = top_sc[1, 0:1, :]
            top = a0 + b0
            z = jnp.zeros((1, TB), F32)
            tau = top
            for kk in range(PEER_TOPK):
                tau = jnp.max(x, axis=0, keepdims=True)
                z = z + jnp.exp(tau - top)
                x = jnp.where(x == tau, neg_inf, x)
            tau = 0.5 * (tau + jnp.max(x, axis=0, keepdims=True))
            thr = tau - s_pair[0]
            cnt = jnp.zeros((N_KEYS, TB), F32)
            for kk in range(PEER_TOPK):
                cnt = jnp.where(top_sc[1, kk:kk + 1, :] >= thr, float(kk + 1), cnt)
            r1_sc[hd] = rank1.astype(BF16)
            e1_sc[hd] = jnp.exp(s_pair[1] - b0).astype(BF16)
            cnt_sc[hd] = cnt
            e0_sc[hd] = jnp.exp(s_pair[0] - a0) / z
            return carry

        lax.fori_loop(0, PEER_HEADS, head, 0)
        acc_sc[...] = jnp.zeros_like(acc_sc)

    act_bufs = (act0_sc, act1_sc)
    vt_refs = (vta_ref, vtb_ref)
    for sub in range(2):
        cur, prev = sub, 1 - sub
        act_bufs[cur][...] = lax.dot_general(u_ref[sub * EB:(sub + 1) * EB, :], xn_sc[...], (((1,), (1,)), ((), ())),
                                             preferred_element_type=F32)
        blk = jnp.clip(2 * g + sub - 1, 0, NBLK - 1)
        live = (2 * g + sub - 1 == blk).astype(F32)
        parts = []
        for ii in range(NI):
            i_glob = blk * NI + ii
            act = act_bufs[prev][ii * N_KEYS:(ii + 1) * N_KEYS, :]
            act = 0.5 * act * (1.0 + lax.erf(act * (2.0 ** -0.5)))
            wgt = jnp.zeros((N_KEYS, TB), BF16)
            for hd in range(PEER_HEADS):
                cnt = (cnt_sc[hd, pl.ds(i_glob, 1), :] * live).astype(BF16)
                e0 = e0_sc[hd, pl.ds(i_glob, 1), :].astype(BF16)
                wgt = wgt + jnp.where(r1_sc[hd] < cnt, e1_sc[hd], jnp.zeros((), BF16)) * e0
            parts.append(act.astype(BF16) * wgt)
        acc_sc[...] += jnp.dot(vt_refs[sub][...], jnp.concatenate(parts, axis=0), preferred_element_type=F32)

    @pl.when(g == ng - 1)
    def _fin():
        o_ref[...] = _rms(h_ref[...] + acc_sc[...].T, gfin_ref[...])


def _peer(h2d, gf, wq, keys, u_bf, vt_bf, gfin, tb, ni):
    rows = h2d.shape[0]
    n_exp = u_bf.shape[0]
    eb = ni * N_KEYS
    nblk = n_exp // eb
    last = nblk // 2 - 1
    return pl.pallas_call(
        functools.partial(_peer_kernel, tb, ni, nblk),
        grid=(rows // tb, nblk // 2 + 1),
        in_specs=[pl.BlockSpec((tb, D_MODEL), lambda i, e: (i, 0)),
                  pl.BlockSpec((1, D_MODEL), lambda i, e: (0, 0)),
                  pl.BlockSpec(wq.shape, lambda i, e: (0, 0)),
                  pl.BlockSpec(keys.shape, lambda i, e: (0, 0, 0)),
                  pl.BlockSpec((2 * eb, D_MODEL), lambda i, e: (jnp.minimum(e, last), 0)),
                  pl.BlockSpec((D_MODEL, eb), lambda i, e: (0, jnp.clip(2 * e - 1, 0, nblk - 1))),
                  pl.BlockSpec((D_MODEL, eb), lambda i, e: (0, jnp.clip(2 * e, 0, nblk - 1))),
                  pl.BlockSpec((1, D_MODEL), lambda i, e: (0, 0))],
        out_specs=pl.BlockSpec((tb, D_MODEL), lambda i, e: (i, 0)),
        out_shape=jax.ShapeDtypeStruct((rows, D_MODEL), F32),
        scratch_shapes=[pltpu.VMEM((tb, D_MODEL), BF16),
                        pltpu.VMEM((2 * PEER_HEADS, tb, LANES), BF16),
                        pltpu.VMEM((PEER_HEADS, N_KEYS, tb), BF16),
                        pltpu.VMEM((PEER_HEADS, N_KEYS, tb), BF16),
                        pltpu.VMEM((PEER_HEADS, N_KEYS, tb), F32),
                        pltpu.VMEM((PEER_HEADS, N_KEYS, tb), F32),
                        pltpu.VMEM((2, -(-N_RANK // 8) * 8, tb), F32),
                        pltpu.VMEM((_CAND_ROWS, tb), F32),
                        pltpu.VMEM((D_MODEL, tb), F32),
                        pltpu.VMEM((eb, tb), F32), pltpu.VMEM((eb, tb), F32)],
        compiler_params=pltpu.CompilerParams(dimension_semantics=("arbitrary", "arbitrary"),
                                             vmem_limit_bytes=VMEM_LIMIT),
        name="peer",
    )(h2d, gf, wq, keys, u_bf, vt_bf, vt_bf, gfin)


def _tile(rows, cap):
    t = cap
    while t > 8 and rows % t:
        t //= 2
    return t if rows % t == 0 else rows


def _stream(x3, init, chunk, npad, wts, run_ffn):
    (norm_mix, w_proj, consts, wba, wbb, wo, norm_ffn, wq, keys, u_bf, vt_bf, norm_final) = wts
    B, L, _ = x3.shape
    x2 = x3.reshape(B * L, D_MODEL)
    proj = _norm_proj(x2, norm_mix, w_proj, _tile(B * L, 1024))
    outs = _mixers(proj.reshape(B, L, N_PROJ), init, consts, chunk, npad)
    oa, ob, s_new, tail_a, c_new, n_new, m_new, tail_b = outs
    states = (tail_a, tail_b, s_new, c_new, n_new, m_new)
    if not run_ffn:
        return None, states
    h2 = _out_proj(oa.reshape(B * L, -1), ob.reshape(B * L, -1), proj, x2, wba, wbb, wo, _tile(B * L, 512))
    y = _peer(h2, norm_ffn, wq, keys, u_bf, vt_bf, norm_final, _tile(B * L, 512), 4)
    return y.reshape(B, L, D_MODEL), states


def kernel(x_prompt, x_sample, state_gdn_S, state_gdn_conv, state_mlstm_C, state_mlstm_n, state_mlstm_m, state_mlstm_conv, meta_tokens, norm_mix, w_in, conv_gdn, gdn_a_log, gdn_dt_bias, gdn_out_norm, conv_mlstm, mlstm_i_bias, mlstm_f_bias, mlstm_out_norm, w_branch, w_out, norm_ffn, peer_w_query, peer_sub_keys, peer_u, peer_v, norm_final):
    assert w_in.shape[0] == 1, "one trunk layer"
    w = w_in[0]
    w_proj = jnp.concatenate(
        [w[:, 0:1536], w[:, 1544:2056], w[:, 2056:3080], w[:, 3080:3592], w[:, 3600:4112], w[:, 4112:6160],
         w[:, 1536:1544], w[:, 3592:3600], jnp.zeros((D_MODEL, N_PROJ - C_SMALL - 4 * HEADS), w.dtype)],
        axis=1).astype(BF16)

    def lane_row(vec, start):
        return jnp.zeros((LANES,), F32).at[start:start + HEADS].set(vec.astype(F32))

    prm = jnp.zeros((8, LANES), F32)
    prm = prm.at[0].set(lane_row(gdn_dt_bias[0], L_A)).at[1].set(lane_row(gdn_a_log[0], L_A))
    prm = prm.at[2].set(lane_row(mlstm_i_bias[0], L_I)).at[3].set(lane_row(mlstm_f_bias[0], L_F))
    consts = (conv_gdn[0], conv_mlstm[0], prm, gdn_out_norm[0].reshape(1, DV), mlstm_out_norm[0])
    wb = w_branch[0].astype(BF16)
    wts = (norm_mix[0].reshape(1, D_MODEL), w_proj, consts, wb[:HEADS * DV], wb[HEADS * DV:], w_out[0].astype(BF16),
           norm_ffn[0].reshape(1, D_MODEL), peer_w_query[0].astype(BF16),
           peer_sub_keys[0].reshape(2 * PEER_HEADS, N_KEYS, LANES).astype(BF16),
           peer_u[0].astype(BF16), peer_v[0].astype(BF16).T, norm_final.reshape(1, D_MODEL))

    def pad_tail(t):
        return jnp.pad(t, ((0, 0), (TAIL - (CONV_W - 1), 0), (0, 0)))

    def lanes(m):
        return jnp.broadcast_to(m[..., None], m.shape + (LANES,))

    wa = conv_gdn.shape[-1]
    wbw = conv_mlstm.shape[-1]
    zero_init = (jnp.zeros((1, TAIL, wa), F32), jnp.zeros((1, TAIL, wbw), F32),
                 jnp.zeros((1, HEADS, DV, DK), F32), jnp.zeros((1, HEADS, DV, DK), F32),
                 jnp.zeros((1, HEADS, DK), F32), jnp.zeros((1, HEADS, LANES), F32))
    x_meta = jnp.concatenate([jnp.zeros((CHUNK - N_META, D_MODEL), F32), meta_tokens.astype(F32)], axis=0)[None]
    _, meta_states = _stream(x_meta, zero_init, CHUNK, CHUNK - N_META, wts, False)
    y_p, st_p = _stream(x_prompt, meta_states, CHUNK, 0, wts, True)
    samp_init = (pad_tail(state_gdn_conv[0]), pad_tail(state_mlstm_conv[0]), state_gdn_S[0], state_mlstm_C[0],
                 state_mlstm_n[0], lanes(state_mlstm_m[0]))
    y_s, st_s = _stream(x_sample, samp_init, x_sample.shape[1], 0, wts, True)

    def pack(st):
        tail_a, tail_b, s_new, c_new, n_new, m_new = st
        return (s_new[None], tail_a[:, TAIL - (CONV_W - 1):][None], c_new[None], n_new[None], m_new[:, :, 0][None],
                tail_b[:, TAIL - (CONV_W - 1):][None])

    return (y_p, y_s) + pack(st_p) + pack(st_s)
```

```python
import functools

import jax
import jax.numpy as jnp
from jax import lax
from jax.experimental import pallas as pl
from jax.experimental.pallas import tpu as pltpu

F32 = jnp.float32
BF16 = jnp.bfloat16
HIGHEST = lax.Precision.HIGHEST

RMS_EPS = 1e-6
D_MODEL = 1024
N_META = 16
CHUNK = 64
HEADS = 4
DK = 128
DV = 128
PEER_HEADS = 8
N_KEYS = 128
PEER_TOPK = 16
LANES = 128
TAIL = 8
CONV_W = 4
VMEM_LIMIT = 56 * 1024 * 1024
MIX_MAX_GROUPS = 4
MIX_MAX_SEQS = 16

C_QKVA, C_ZA, C_QKB, C_VB, C_OB, C_GA, C_GB, C_SMALL = 0, 1536, 2048, 3072, 3584, 4096, 5120, 6144
N_PROJ = 6272
PROJ_TN = 896
L_A, L_B, L_I, L_F = 0, 4, 8, 12


def _rms(x, gain):
    return x * lax.rsqrt(jnp.mean(x * x, axis=-1, keepdims=True) + RMS_EPS) * gain


def _mm(a, b):
    return jnp.dot(a.astype(BF16), b.astype(BF16), preferred_element_type=F32)


def _mm_nt(a, b):
    return lax.dot_general(a.astype(BF16), b.astype(BF16), (((1,), (1,)), ((), ())), preferred_element_type=F32)


def _mm_tn(a, b):
    return lax.dot_general(a.astype(BF16), b.astype(BF16), (((0,), (0,)), ((), ())), preferred_element_type=F32)


def _mm_hi(a, b):
    return jnp.dot(a, b, preferred_element_type=F32, precision=HIGHEST)


def _norm_proj_kernel(x_ref, g_ref, w_ref, o_ref, xn_ref):
    @pl.when(pl.program_id(1) == 0)
    def _():
        xn_ref[...] = _rms(x_ref[...], g_ref[...]).astype(BF16)

    o_ref[...] = jnp.dot(xn_ref[...], w_ref[...], preferred_element_type=F32)


def _norm_proj(x2d, gain, w_bf16, tm):
    rows = x2d.shape[0]
    return pl.pallas_call(
        _norm_proj_kernel,
        grid=(rows // tm, N_PROJ // PROJ_TN),
        in_specs=[pl.BlockSpec((tm, D_MODEL), lambda i, j: (i, 0)),
                  pl.BlockSpec((1, D_MODEL), lambda i, j: (0, 0)),
                  pl.BlockSpec((D_MODEL, PROJ_TN), lambda i, j: (0, j))],
        out_specs=pl.BlockSpec((tm, PROJ_TN), lambda i, j: (i, j)),
        out_shape=jax.ShapeDtypeStruct((rows, N_PROJ), F32),
        scratch_shapes=[pltpu.VMEM((tm, D_MODEL), BF16)],
        compiler_params=pltpu.CompilerParams(dimension_semantics=("arbitrary", "arbitrary"),
                                             vmem_limit_bytes=VMEM_LIMIT),
        name="norm_proj",
    )(x2d, gain, w_bf16)


def _causal_conv(x, p8, w):
    xe = jnp.concatenate([p8, x], axis=0)
    y = x * w[CONV_W - 1:CONV_W]
    for k in range(1, CONV_W):
        y = y + pltpu.roll(xe, k, 0)[TAIL:] * w[CONV_W - 1 - k:CONV_W - k]
    return y, xe[x.shape[0]:]


def _mixer_kernel(C, nc, npad, NB, NBI,
                  qkva_ref, za_ref, qkb_ref, vb_ref, og_ref, sm_ref,
                  ta0_ref, tb0_ref, s0_ref, c0_ref, n0_ref, m0_ref,
                  cwa_ref, cwb_ref, prm_ref, gnorm_ref, mnorm_ref,
                  oa_ref, ob_ref, so_ref, tao_ref, co_ref, no_ref, mo_ref, tbo_ref,
                  s_sc, c_sc, n_sc, m_sc, pa_sc, pb_sc):
    ci = pl.program_id(1)
    GS = (2 * LANES) // (HEADS * C)
    NBLK = GS * HEADS
    R = NBLK * C
    T = GS * C
    log2c = C.bit_length() - 1

    def rows(b):
        return slice(b * C, (b + 1) * C)

    def stack(pieces):
        return jnp.concatenate(pieces, axis=0)

    def wide(x):
        return jnp.concatenate([x] * (R // LANES), axis=1)

    @pl.when(ci == 0)
    def _init():
        for n in range(NB):
            ni = min(n, NBI - 1)
            s_sc[n] = s0_ref[ni]
            c_sc[n] = c0_ref[ni]
            n_sc[n] = n0_ref[ni]
            pa_sc[n] = ta0_ref[ni]
            pb_sc[n] = tb0_ref[ni]
            for h in range(HEADS):
                m_sc[rows(n * HEADS + h), :] = jnp.broadcast_to(m0_ref[ni, h:h + 1, :], (C, LANES))

    def group(grp):
        seqs = list(range(grp * GS, (grp + 1) * GS))
        blocks = [(n, h) for n in seqs for h in range(HEADS)]
        mrows = slice(grp * R, (grp + 1) * R)
        rowi = lax.broadcasted_iota(jnp.int32, (R, R), 0)
        coli = lax.broadcasted_iota(jnp.int32, (R, R), 1)
        same = (rowi >> log2c) == (coli >> log2c)
        incl = same & (rowi >= coli)
        strict = same & (rowi > coli)
        neg_inf = -jnp.inf

        qkva, qkb = {}, {}
        for n in seqs:
            ya, tail_a = _causal_conv(qkva_ref[n], pa_sc[n], cwa_ref[...])
            yb, tail_b = _causal_conv(qkb_ref[n], pb_sc[n], cwb_ref[...])
            pa_sc[n] = tail_a
            pb_sc[n] = tail_b
            qkva[n] = jax.nn.silu(ya)
            qkb[n] = jax.nn.silu(yb)
        hd = HEADS * DK
        q_all = stack([qkva[n][:, h * DK:(h + 1) * DK] for n, h in blocks])
        k_all = stack([qkva[n][:, hd + h * DK:hd + (h + 1) * DK] for n, h in blocks])
        v_all = stack([qkva[n][:, 2 * hd + h * DV:2 * hd + (h + 1) * DV] for n, h in blocks])
        qb_all = stack([qkb[n][:, h * DK:(h + 1) * DK] for n, h in blocks])
        kb_all = stack([qkb[n][:, hd + h * DK:hd + (h + 1) * DK] for n, h in blocks]) * (DK ** -0.5)
        vb_all = stack([vb_ref[n, :, h * DV:(h + 1) * DV] for n, h in blocks])

        sm = stack([sm_ref[n] for n in seqs])
        lane = lax.broadcasted_iota(jnp.int32, (T, LANES), 1)
        g_full = -jnp.exp(prm_ref[1:2]) * jax.nn.softplus(sm + prm_ref[0:1])
        beta_full = jax.nn.sigmoid(sm)
        logi_full = sm + prm_ref[2:3]
        logf_full = jax.nn.log_sigmoid(sm + prm_ref[3:4])
        if npad:
            real = (lax.broadcasted_iota(jnp.int32, (T, LANES), 0) & (C - 1)) >= npad
            g_full = jnp.where(real, g_full, 0.0)
            beta_full = jnp.where(real, beta_full, 0.0)
            logf_full = jnp.where(real, logf_full, 0.0)
        cs_in = jnp.where(lane < L_A + HEADS, g_full,
                          jnp.where((lane >= L_F) & (lane < L_F + HEADS), logf_full, 0.0))
        ti = lax.broadcasted_iota(jnp.int32, (T, T), 0)
        tj = lax.broadcasted_iota(jnp.int32, (T, T), 1)
        tril = (((ti >> log2c) == (tj >> log2c)) & (ti >= tj)).astype(F32)
        cums = _mm_hi(tril, cs_in)
        yield
        gates = jnp.where((lane >= L_B) & (lane < L_B + HEADS), beta_full,
                          jnp.where((lane >= L_I) & (lane < L_I + HEADS), logi_full, cums))
        last = stack([jnp.broadcast_to(cums[k * C + C - 1:k * C + C, :], (C, LANES)) for k in range(GS)])

        def col(src, base):
            return stack([jnp.broadcast_to(src[(n - seqs[0]) * C:(n - seqs[0] + 1) * C, base + h:base + h + 1], (C, LANES))
                          for n, h in blocks])

        g_c, beta_c, li_c, f_c = col(gates, L_A), col(gates, L_B), col(gates, L_I), col(gates, L_F)
        glast_c, fend_c = col(last, L_A), col(last, L_F)
        lane_r = lax.broadcasted_iota(jnp.int32, (R, LANES), 1)
        rowform = jnp.where(lane_r == 0, g_c, jnp.where(lane_r == 1, li_c, f_c)).T
        g_r, li_r, f_r = rowform[0:1, :], rowform[1:2, :], rowform[2:3, :]
        if npad:
            li_c = jnp.where((lax.broadcasted_iota(jnp.int32, (R, LANES), 0) & (C - 1)) >= npad, li_c, neg_inf)
            li_r = jnp.where((coli[0:1, :] & (C - 1)) >= npad, li_r, neg_inf)

        q_all = q_all * lax.rsqrt(jnp.sum(q_all * q_all, axis=-1, keepdims=True) + RMS_EPS) * (DK ** -0.5)
        k_all = k_all * lax.rsqrt(jnp.sum(k_all * k_all, axis=-1, keepdims=True) + RMS_EPS)
        decay = jnp.exp(jnp.where(incl, wide(g_c) - g_r, neg_inf))
        low = jnp.where(strict, wide(beta_c) * decay * _mm_nt(k_all, k_all), 0.0)
        yield
        eye = (rowi == coli).astype(F32)
        xinv = eye - jnp.where(((rowi & 1) == 1) & (coli == rowi - 1), low, 0.0)
        shift = 1
        while (1 << shift) < C:
            br = rowi >> shift
            bc = coli >> shift
            cm = jnp.where(((br & 1) == 1) & (bc == br - 1), low, 0.0)
            t = _mm(xinv, cm)
            yield
            xinv = xinv - _mm(t, xinv)
            yield
            shift += 1
        eg = jnp.exp(g_c)
        u_all = _mm(xinv, beta_c * v_all)
        w_all = _mm(xinv, (beta_c * eg) * k_all)
        yield
        qk = _mm_nt(q_all, k_all) * decay
        yield
        q_dec = q_all * eg
        k_end = k_all * jnp.exp(glast_c - g_c)
        g_end = jnp.exp(glast_c)
        deltas, inter = [], []
        for b, (n, h) in enumerate(blocks):
            s = s_sc[n, h]
            delta = u_all[rows(b)] - _mm_nt(w_all[rows(b)], s)
            inter.append(_mm_nt(q_dec[rows(b)], s))
            s_sc[n, h] = g_end[b * C:b * C + 1, :] * s + _mm_tn(delta, k_end[rows(b)])
            deltas.append(delta)
            yield
        o_all = stack(inter) + _mm(qk, stack(deltas))
        yield
        za_all = stack([za_ref[n, :, h * DV:(h + 1) * DV] for n, h in blocks])
        o_all = (_rms(o_all, gnorm_ref[...]) * jax.nn.silu(za_all)).astype(BF16)
        for b, (n, h) in enumerate(blocks):
            oa_ref[n, :, h * DV:(h + 1) * DV] = o_all[rows(b)]

        dmat = jnp.where(incl, wide(f_c) - f_r + li_r, neg_inf)
        dmax = jnp.broadcast_to(jnp.max(dmat, axis=1, keepdims=True), (R, LANES))
        qkm = _mm_nt(qb_all, kb_all)
        yield
        ms = m_sc[mrows, :]
        m_inter = f_c + ms
        m_t = jnp.maximum(m_inter, dmax)
        w_intra = jnp.exp(dmat - wide(m_t)) * qkm
        w_inter = jnp.exp(m_inter - m_t)
        e_col = fend_c - f_c + li_c
        e_max = jnp.max(e_col.reshape(NBLK, C, LANES), axis=1, keepdims=True)
        e_max = jnp.broadcast_to(e_max, (NBLK, C, LANES)).reshape(R, LANES)
        m_new = jnp.maximum(fend_c + ms, e_max)
        dec = jnp.exp(fend_c + ms - m_new)
        wk = jnp.exp(e_col - m_new)
        wkv = wk * vb_all
        wkk = wk * kb_all
        inter, ns_rows = [], []
        for b, (n, h) in enumerate(blocks):
            cs = c_sc[n, h]
            ns = n_sc[n, h:h + 1, :]
            inter.append(_mm_nt(qb_all[rows(b)], cs))
            ns_rows.append(jnp.broadcast_to(ns, (C, LANES)))
            dec_b = dec[b * C:b * C + 1, :]
            c_sc[n, h] = dec_b * cs + _mm_tn(wkv[rows(b)], kb_all[rows(b)])
            n_sc[n, h:h + 1, :] = dec_b * ns + jnp.sum(wkk[rows(b)], axis=0, keepdims=True)
            yield
        m_sc[mrows, :] = m_new
        yield
        num = w_inter * stack(inter) + _mm(w_intra, vb_all)
        den = (w_inter[:, 0:1] * jnp.sum(qb_all * stack(ns_rows), axis=1, keepdims=True)
               + jnp.sum(w_intra, axis=1, keepdims=True))
        hh = num / jnp.maximum(jnp.abs(den), jnp.exp(-m_t[:, 0:1]))
        mnorm = stack([jnp.broadcast_to(mnorm_ref[h:h + 1, :], (C, DV)) for _, h in blocks])
        og_all = stack([og_ref[n, :, h * DV:(h + 1) * DV] for n, h in blocks])
        hh = (_rms(hh, mnorm) * jax.nn.sigmoid(og_all)).astype(BF16)
        for b, (n, h) in enumerate(blocks):
            ob_ref[n, :, h * DV:(h + 1) * DV] = hh[rows(b)]

    gens = [group(grp) for grp in range(NB // GS)]
    while gens:
        for gen in list(gens):
            try:
                next(gen)
            except StopIteration:
                gens.remove(gen)

    @pl.when(ci == nc - 1)
    def _fin():
        so_ref[...] = s_sc[...]
        co_ref[...] = c_sc[...]
        no_ref[...] = n_sc[...]
        tao_ref[...] = pa_sc[...]
        tbo_ref[...] = pb_sc[...]
        for n in range(NB):
            for h in range(HEADS):
                b = n * HEADS + h
                mo_ref[n, h:h + 1, :] = m_sc[b * C:b * C + 1, :]


def _mixers(proj3, init, consts, C, npad, groups):
    B, L, _ = proj3.shape
    nc = L // C
    assert C & (C - 1) == 0 and (2 * LANES) % (HEADS * C) == 0, "chunk must stack into 2*LANES rows"
    NB = groups * ((2 * LANES) // (HEADS * C))
    assert B % NB == 0
    cwa, cwb, prm, gnorm, mnorm = consts
    wa = cwa.shape[1]
    wb = cwb.shape[1]
    tail_a, tail_b, s0, c0, n0, m0 = init
    shared_init = s0.shape[0] == 1 and B > 1
    assert shared_init or s0.shape[0] == B
    NBI = 1 if shared_init else NB
    wh = HEADS * DV

    def col(width, start):
        return pl.BlockSpec((NB, C, width), lambda b, c, _s=start // width: (b, c, _s))

    def st(shape):
        nd = len(shape)
        if shared_init:
            return pl.BlockSpec((1,) + shape, lambda b, c, _n=nd: (0,) * (_n + 1))
        return pl.BlockSpec((NB,) + shape, lambda b, c, _n=nd: (b,) + (0,) * _n)

    def full(a):
        return pl.BlockSpec(a.shape, lambda b, c, _n=a.ndim: (0,) * _n)

    def outst(shape):
        return pl.BlockSpec((NB,) + shape, lambda b, c, _n=len(shape): (b,) + (0,) * _n)

    in_specs = [col(wa, C_QKVA), col(wh, C_ZA), col(wb, C_QKB), col(wh, C_VB), col(wh, C_OB), col(LANES, C_SMALL),
                st((TAIL, wa)), st((TAIL, wb)), st((HEADS, DV, DK)), st((HEADS, DV, DK)), st((HEADS, DK)), st((HEADS, LANES)),
                full(cwa), full(cwb), full(prm), full(gnorm), full(mnorm)]
    out_specs = [pl.BlockSpec((NB, C, wh), lambda b, c: (b, c, 0)),
                 pl.BlockSpec((NB, C, wh), lambda b, c: (b, c, 0)),
                 outst((HEADS, DV, DK)), outst((TAIL, wa)), outst((HEADS, DV, DK)), outst((HEADS, DK)),
                 outst((HEADS, LANES)), outst((TAIL, wb))]
    out_shape = [jax.ShapeDtypeStruct((B, L, wh), BF16), jax.ShapeDtypeStruct((B, L, wh), BF16),
                 jax.ShapeDtypeStruct((B, HEADS, DV, DK), F32), jax.ShapeDtypeStruct((B, TAIL, wa), F32),
                 jax.ShapeDtypeStruct((B, HEADS, DV, DK), F32), jax.ShapeDtypeStruct((B, HEADS, DK), F32),
                 jax.ShapeDtypeStruct((B, HEADS, LANES), F32), jax.ShapeDtypeStruct((B, TAIL, wb), F32)]
    scratch = [pltpu.VMEM((NB, HEADS, DV, DK), F32), pltpu.VMEM((NB, HEADS, DV, DK), F32),
               pltpu.VMEM((NB, HEADS, DK), F32), pltpu.VMEM((NB * HEADS * C, LANES), F32),
               pltpu.VMEM((NB, TAIL, wa), F32), pltpu.VMEM((NB, TAIL, wb), F32)]
    return pl.pallas_call(
        functools.partial(_mixer_kernel, C, nc, npad, NB, NBI),
        grid=(B // NB, nc),
        in_specs=in_specs, out_specs=out_specs, out_shape=out_shape, scratch_shapes=scratch,
        compiler_params=pltpu.CompilerParams(dimension_semantics=("arbitrary", "arbitrary"),
                                             vmem_limit_bytes=VMEM_LIMIT),
        name="mixers",
    )(proj3, proj3, proj3, proj3, proj3, proj3, tail_a, tail_b, s0, c0, n0, m0, cwa, cwb, prm, gnorm, mnorm)


def _out_proj_kernel(oa_ref, ob_ref, ga_ref, gb_ref, h_ref, wba_ref, wbb_ref, wo_ref, o_ref):
    ya = jnp.dot(oa_ref[...], wba_ref[...], preferred_element_type=F32)
    yb = jnp.dot(ob_ref[...], wbb_ref[...], preferred_element_type=F32)
    merged = jax.nn.sigmoid(ga_ref[...]) * ya + jax.nn.sigmoid(gb_ref[...]) * yb
    o_ref[...] = h_ref[...] + jnp.dot(merged.astype(BF16), wo_ref[...], preferred_element_type=F32)


def _out_proj(oa, ob, proj, h2d, wba, wbb, wo, tm):
    rows = h2d.shape[0]
    w_half = HEADS * DV

    def rowblk(width, cidx=0):
        return pl.BlockSpec((tm, width), lambda i, _c=cidx: (i, _c))

    def full(a):
        return pl.BlockSpec(a.shape, lambda i: (0, 0))

    return pl.pallas_call(
        _out_proj_kernel,
        grid=(rows // tm,),
        in_specs=[rowblk(w_half), rowblk(w_half), rowblk(D_MODEL, C_GA // D_MODEL), rowblk(D_MODEL, C_GB // D_MODEL),
                  rowblk(D_MODEL), full(wba), full(wbb), full(wo)],
        out_specs=rowblk(D_MODEL),
        out_shape=jax.ShapeDtypeStruct((rows, D_MODEL), F32),
        compiler_params=pltpu.CompilerParams(dimension_semantics=("arbitrary",), vmem_limit_bytes=VMEM_LIMIT),
        name="out_proj",
    )(oa, ob, proj, proj, h2d, wba, wbb, wo)


N_RANK = PEER_TOPK + 1
_STAIR = [(k, N_RANK // (k + 1)) for k in range(N_RANK)]
_N_CAND = sum(n for _, n in _STAIR)
_CAND_ROWS = -(-_N_CAND // 8) * 8


def _peer_kernel(TB, NI, NBLK,
                 h_ref, gf_ref, wq_ref, keys_ref, u_ref, vta_ref, vtb_ref, gfin_ref, o_ref,
                 xn_sc, q_sc, r1_sc, e1_sc, cnt_sc, e0_sc, acc_sc, act0_sc, act1_sc):
    g = pl.program_id(1)
    ng = pl.num_programs(1)
    EB = NI * N_KEYS
    neg_inf = -jnp.inf

    @pl.when(g == 0)
    def _prep():
        act0_sc[...] = jnp.zeros_like(act0_sc)
        act1_sc[...] = jnp.zeros_like(act1_sc)
        xn = _rms(h_ref[...], gf_ref[...]).astype(BF16)
        xn_sc[...] = xn
        q = jnp.dot(xn, wq_ref[...], preferred_element_type=F32)
        for j in range(2 * PEER_HEADS):
            q_sc[j] = q[:, j * LANES:(j + 1) * LANES].astype(BF16)

        def head(hd, carry):
            s_lists = [lax.dot_general(keys_ref[2 * hd + p], q_sc[2 * hd + p], (((1,), (1,)), ((), ())),
                                       preferred_element_type=F32) for p in range(2)]
            rank1_t, e1_t, cnt_t, e0_t = [], [], [], []
            for lt in range(TB // LANES):
                tok = slice(lt * LANES, (lt + 1) * LANES)
                s0 = s_lists[0][:, tok]
                s1 = s_lists[1][:, tok]
                tops = []
                rank1 = jnp.full((N_KEYS, LANES), float(N_KEYS - 1), F32)
                for p, x in ((1, s1), (0, s0)):
                    vals = []
                    for kk in range(N_RANK):
                        mx = jnp.max(x, axis=0, keepdims=True)
                        vals.append(mx)
                        hit = x == mx
                        if p == 1:
                            rank1 = jnp.where(hit, float(kk), rank1)
                        x = jnp.where(hit, neg_inf, x)
                    tops.append(vals)
                top1, top0 = tops
                top1_arr = jnp.concatenate(top1, axis=0)
                cand = jnp.concatenate([top0[kk] + top1_arr[0:n] for kk, n in _STAIR]
                                       + [jnp.full((_CAND_ROWS - _N_CAND, LANES), neg_inf, F32)], axis=0)
                top = top0[0] + top1[0]
                z = jnp.zeros((1, LANES), F32)
                tau = top
                x = cand
                for kk in range(PEER_TOPK):
                    tau = jnp.max(x, axis=0, keepdims=True)
                    z = z + jnp.exp(tau - top)
                    x = jnp.where(x == tau, neg_inf, x)
                tau = 0.5 * (tau + jnp.max(x, axis=0, keepdims=True))
                thr = tau - s0
                cnt = jnp.zeros((N_KEYS, LANES), F32)
                for kk in range(PEER_TOPK):
                    cnt = jnp.where(top1[kk] >= thr, float(kk + 1), cnt)
                rank1_t.append(rank1.astype(BF16))
                e1_t.append(jnp.exp(s1 - top1[0]).astype(BF16))
                cnt_t.append(cnt)
                e0_t.append(jnp.exp(s0 - top0[0]) / z)
            r1_sc[hd] = jnp.concatenate(rank1_t, axis=1)
            e1_sc[hd] = jnp.concatenate(e1_t, axis=1)
            cnt_sc[hd] = jnp.concatenate(cnt_t, axis=1)
            e0_sc[hd] = jnp.concatenate(e0_t, axis=1)
            return carry

        lax.fori_loop(0, PEER_HEADS, head, 0)
        acc_sc[...] = jnp.zeros_like(acc_sc)

    act_bufs = (act0_sc, act1_sc)
    vt_refs = (vta_ref, vtb_ref)
    for sub in range(2):
        cur, prev = sub, 1 - sub
        act_bufs[cur][...] = lax.dot_general(u_ref[sub * EB:(sub + 1) * EB, :], xn_sc[...], (((1,), (1,)), ((), ())),
                                             preferred_element_type=F32)
        blk = jnp.clip(2 * g + sub - 1, 0, NBLK - 1)
        live = (2 * g + sub - 1 == blk).astype(F32)
        parts = []
        for ii in range(NI):
            i_glob = blk * NI + ii
            act = act_bufs[prev][ii * N_KEYS:(ii + 1) * N_KEYS, :]
            act = 0.5 * act * (1.0 + lax.erf(act * (2.0 ** -0.5)))
            wgt = jnp.zeros((N_KEYS, TB), BF16)
            for hd in range(PEER_HEADS):
                cnt = (cnt_sc[hd, pl.ds(i_glob, 1), :] * live).astype(BF16)
                e0 = e0_sc[hd, pl.ds(i_glob, 1), :].astype(BF16)
                wgt = wgt + jnp.where(r1_sc[hd] < cnt, e1_sc[hd], jnp.zeros((), BF16)) * e0
            parts.append(act.astype(BF16) * wgt)
        acc_sc[...] += jnp.dot(vt_refs[sub][...], jnp.concatenate(parts, axis=0), preferred_element_type=F32)

    @pl.when(g == ng - 1)
    def _fin():
        o_ref[...] = _rms(h_ref[...] + acc_sc[...].T, gfin_ref[...])


def _peer(h2d, gf, wq, keys, u_bf, vt_bf, gfin, tb, ni):
    rows = h2d.shape[0]
    n_exp = u_bf.shape[0]
    eb = ni * N_KEYS
    nblk = n_exp // eb
    last = nblk // 2 - 1
    return pl.pallas_call(
        functools.partial(_peer_kernel, tb, ni, nblk),
        grid=(rows // tb, nblk // 2 + 1),
        in_specs=[pl.BlockSpec((tb, D_MODEL), lambda i, e: (i, 0)),
                  pl.BlockSpec((1, D_MODEL), lambda i, e: (0, 0)),
                  pl.BlockSpec(wq.shape, lambda i, e: (0, 0)),
                  pl.BlockSpec(keys.shape, lambda i, e: (0, 0, 0)),
                  pl.BlockSpec((2 * eb, D_MODEL), lambda i, e: (jnp.minimum(e, last), 0)),
                  pl.BlockSpec((D_MODEL, eb), lambda i, e: (0, jnp.clip(2 * e - 1, 0, nblk - 1))),
                  pl.BlockSpec((D_MODEL, eb), lambda i, e: (0, jnp.clip(2 * e, 0, nblk - 1))),
                  pl.BlockSpec((1, D_MODEL), lambda i, e: (0, 0))],
        out_specs=pl.BlockSpec((tb, D_MODEL), lambda i, e: (i, 0)),
        out_shape=jax.ShapeDtypeStruct((rows, D_MODEL), F32),
        scratch_shapes=[pltpu.VMEM((tb, D_MODEL), BF16),
                        pltpu.VMEM((2 * PEER_HEADS, tb, LANES), BF16),
                        pltpu.VMEM((PEER_HEADS, N_KEYS, tb), BF16),
                        pltpu.VMEM((PEER_HEADS, N_KEYS, tb), BF16),
                        pltpu.VMEM((PEER_HEADS, N_KEYS, tb), F32),
                        pltpu.VMEM((PEER_HEADS, N_KEYS, tb), F32),
                        pltpu.VMEM((D_MODEL, tb), F32),
                        pltpu.VMEM((eb, tb), F32), pltpu.VMEM((eb, tb), F32)],
        compiler_params=pltpu.CompilerParams(dimension_semantics=("arbitrary", "arbitrary"),
                                             vmem_limit_bytes=VMEM_LIMIT),
        name="peer",
    )(h2d, gf, wq, keys, u_bf, vt_bf, vt_bf, gfin)


def _tile(rows, cap):
    t = cap
    while t > 8 and rows % t:
        t //= 2
    return t if rows % t == 0 else rows


def _mixer_groups(batch, chunk):
    per_group = (2 * LANES) // (HEADS * chunk)
    groups = MIX_MAX_GROUPS
    while groups > 1 and (batch % (groups * per_group) or groups * per_group > MIX_MAX_SEQS):
        groups //= 2
    return groups


def _stream(x3, init, chunk, npad, wts, run_ffn):
    (norm_mix, w_proj, consts, wba, wbb, wo, norm_ffn, wq, keys, u_bf, vt_bf, norm_final) = wts
    B, L, _ = x3.shape
    x2 = x3.reshape(B * L, D_MODEL)
    proj = _norm_proj(x2, norm_mix, w_proj, _tile(B * L, 1024))
    outs = _mixers(proj.reshape(B, L, N_PROJ), init, consts, chunk, npad, _mixer_groups(B, chunk))
    oa, ob, s_new, tail_a, c_new, n_new, m_new, tail_b = outs
    states = (tail_a, tail_b, s_new, c_new, n_new, m_new)
    if not run_ffn:
        return None, states
    h2 = _out_proj(oa.reshape(B * L, -1), ob.reshape(B * L, -1), proj, x2, wba, wbb, wo, _tile(B * L, 512))
    y = _peer(h2, norm_ffn, wq, keys, u_bf, vt_bf, norm_final, _tile(B * L, 512), 8)
    return y.reshape(B, L, D_MODEL), states


def kernel(x_prompt, x_sample, state_gdn_S, state_gdn_conv, state_mlstm_C, state_mlstm_n, state_mlstm_m, state_mlstm_conv, meta_tokens, norm_mix, w_in, conv_gdn, gdn_a_log, gdn_dt_bias, gdn_out_norm, conv_mlstm, mlstm_i_bias, mlstm_f_bias, mlstm_out_norm, w_branch, w_out, norm_ffn, peer_w_query, peer_sub_keys, peer_u, peer_v, norm_final):
    assert w_in.shape[0] == 1, "one trunk layer"
    w = w_in[0]
    w_proj = jnp.concatenate(
        [w[:, 0:1536], w[:, 1544:2056], w[:, 2056:3080], w[:, 3080:3592], w[:, 3600:4112], w[:, 4112:6160],
         w[:, 1536:1544], w[:, 3592:3600], jnp.zeros((D_MODEL, N_PROJ - C_SMALL - 4 * HEADS), w.dtype)],
        axis=1).astype(BF16)

    def lane_row(vec, start):
        return jnp.zeros((LANES,), F32).at[start:start + HEADS].set(vec.astype(F32))

    prm = jnp.zeros((8, LANES), F32)
    prm = prm.at[0].set(lane_row(gdn_dt_bias[0], L_A)).at[1].set(lane_row(gdn_a_log[0], L_A))
    prm = prm.at[2].set(lane_row(mlstm_i_bias[0], L_I)).at[3].set(lane_row(mlstm_f_bias[0], L_F))
    consts = (conv_gdn[0], conv_mlstm[0], prm, gdn_out_norm[0].reshape(1, DV), mlstm_out_norm[0])
    wb = w_branch[0].astype(BF16)
    wts = (norm_mix[0].reshape(1, D_MODEL), w_proj, consts, wb[:HEADS * DV], wb[HEADS * DV:], w_out[0].astype(BF16),
           norm_ffn[0].reshape(1, D_MODEL), peer_w_query[0].astype(BF16),
           peer_sub_keys[0].reshape(2 * PEER_HEADS, N_KEYS, LANES).astype(BF16),
           peer_u[0].astype(BF16), peer_v[0].astype(BF16).T, norm_final.reshape(1, D_MODEL))

    def pad_tail(t):
        return jnp.pad(t, ((0, 0), (TAIL - (CONV_W - 1), 0), (0, 0)))

    def lanes(m):
        return jnp.broadcast_to(m[..., None], m.shape + (LANES,))

    wa = conv_gdn.shape[-1]
    wbw = conv_mlstm.shape[-1]
    zero_init = (jnp.zeros((1, TAIL, wa), F32), jnp.zeros((1, TAIL, wbw), F32),
                 jnp.zeros((1, HEADS, DV, DK), F32), jnp.zeros((1, HEADS, DV, DK), F32),
                 jnp.zeros((1, HEADS, DK), F32), jnp.zeros((1, HEADS, LANES), F32))
    x_meta = jnp.concatenate([jnp.zeros((CHUNK - N_META, D_MODEL), F32), meta_tokens.astype(F32)], axis=0)[None]
    _, meta_states = _stream(x_meta, zero_init, CHUNK, CHUNK - N_META, wts, False)
    y_p, st_p = _stream(x_prompt, meta_states, CHUNK, 0, wts, True)
    samp_init = (pad_tail(state_gdn_conv[0]), pad_tail(state_mlstm_conv[0]), state_gdn_S[0], state_mlstm_C[0],
                 state_mlstm_n[0], lanes(state_mlstm_m[0]))
    y_s, st_s = _stream(x_sample, samp_init, x_sample.shape[1], 0, wts, True)

    def pack(st):
        tail_a, tail_b, s_new, c_new, n_new, m_new = st
        return (s_new[None], tail_a[:, TAIL - (CONV_W - 1):][None], c_new[None], n_new[None], m_new[:, :, 0][None],
                tail_b[:, TAIL - (CONV_W - 1):][None])

    return (y_p, y_s) + pack(st_p) + pack(st_s)
```

```python
import functools

import jax
import jax.numpy as jnp
from jax import lax
from jax.experimental import pallas as pl
from jax.experimental.pallas import tpu as pltpu

F32 = jnp.float32
BF16 = jnp.bfloat16
HIGHEST = lax.Precision.HIGHEST

RMS_EPS = 1e-6
D_MODEL = 1024
N_META = 16
CHUNK = 64
HEADS = 4
DK = 128
DV = 128
PEER_HEADS = 8
N_KEYS = 128
PEER_TOPK = 16
LANES = 128
TAIL = 8
CONV_W = 4
VMEM_LIMIT = 56 * 1024 * 1024
MIX_MAX_GROUPS = 4
MIX_MAX_SEQS = 16
PEER_VMEM_LIMIT = 60 * 1024 * 1024
SELECT_STEPS_PER_MAIN_STEP = 12

C_QKVA, C_ZA, C_QKB, C_VB, C_OB, C_GA, C_GB, C_SMALL = 0, 1536, 2048, 3072, 3584, 4096, 5120, 6144
N_PROJ = 6272
PROJ_TN = 896
L_A, L_B, L_I, L_F = 0, 4, 8, 12


def _rms(x, gain):
    return x * lax.rsqrt(jnp.mean(x * x, axis=-1, keepdims=True) + RMS_EPS) * gain


def _mm(a, b):
    return jnp.dot(a.astype(BF16), b.astype(BF16), preferred_element_type=F32)


def _mm_nt(a, b):
    return lax.dot_general(a.astype(BF16), b.astype(BF16), (((1,), (1,)), ((), ())), preferred_element_type=F32)


def _mm_tn(a, b):
    return lax.dot_general(a.astype(BF16), b.astype(BF16), (((0,), (0,)), ((), ())), preferred_element_type=F32)


def _mm_hi(a, b):
    return jnp.dot(a, b, preferred_element_type=F32, precision=HIGHEST)


def _norm_proj_kernel(x_ref, g_ref, w_ref, o_ref, xn_ref):
    @pl.when(pl.program_id(1) == 0)
    def _():
        xn_ref[...] = _rms(x_ref[...], g_ref[...]).astype(BF16)

    o_ref[...] = jnp.dot(xn_ref[...], w_ref[...], preferred_element_type=F32)


def _norm_proj(x2d, gain, w_bf16, tm):
    rows = x2d.shape[0]
    return pl.pallas_call(
        _norm_proj_kernel,
        grid=(rows // tm, N_PROJ // PROJ_TN),
        in_specs=[pl.BlockSpec((tm, D_MODEL), lambda i, j: (i, 0)),
                  pl.BlockSpec((1, D_MODEL), lambda i, j: (0, 0)),
                  pl.BlockSpec((D_MODEL, PROJ_TN), lambda i, j: (0, j))],
        out_specs=pl.BlockSpec((tm, PROJ_TN), lambda i, j: (i, j)),
        out_shape=jax.ShapeDtypeStruct((rows, N_PROJ), F32),
        scratch_shapes=[pltpu.VMEM((tm, D_MODEL), BF16)],
        compiler_params=pltpu.CompilerParams(dimension_semantics=("arbitrary", "arbitrary"),
                                             vmem_limit_bytes=VMEM_LIMIT),
        name="norm_proj",
    )(x2d, gain, w_bf16)


def _causal_conv(x, p8, w):
    xe = jnp.concatenate([p8, x], axis=0)
    y = x * w[CONV_W - 1:CONV_W]
    for k in range(1, CONV_W):
        y = y + pltpu.roll(xe, k, 0)[TAIL:] * w[CONV_W - 1 - k:CONV_W - k]
    return y, xe[x.shape[0]:]


def _mixer_kernel(C, nc, npad, NB, NBI,
                  qkva_ref, za_ref, qkb_ref, vb_ref, og_ref, sm_ref,
                  ta0_ref, tb0_ref, s0_ref, c0_ref, n0_ref, m0_ref,
                  cwa_ref, cwb_ref, prm_ref, gnorm_ref, mnorm_ref,
                  oa_ref, ob_ref, so_ref, tao_ref, co_ref, no_ref, mo_ref, tbo_ref,
                  s_sc, c_sc, n_sc, m_sc, pa_sc, pb_sc):
    ci = pl.program_id(1)
    GS = (2 * LANES) // (HEADS * C)
    NBLK = GS * HEADS
    R = NBLK * C
    T = GS * C
    log2c = C.bit_length() - 1

    def rows(b):
        return slice(b * C, (b + 1) * C)

    def stack(pieces):
        return jnp.concatenate(pieces, axis=0)

    def wide(x):
        return jnp.concatenate([x] * (R // LANES), axis=1)

    @pl.when(ci == 0)
    def _init():
        for n in range(NB):
            ni = min(n, NBI - 1)
            s_sc[n] = s0_ref[ni]
            c_sc[n] = c0_ref[ni]
            n_sc[n] = n0_ref[ni]
            pa_sc[n] = ta0_ref[ni]
            pb_sc[n] = tb0_ref[ni]
            for h in range(HEADS):
                m_sc[rows(n * HEADS + h), :] = jnp.broadcast_to(m0_ref[ni, h:h + 1, :], (C, LANES))

    def group(grp):
        seqs = list(range(grp * GS, (grp + 1) * GS))
        blocks = [(n, h) for n in seqs for h in range(HEADS)]
        mrows = slice(grp * R, (grp + 1) * R)
        rowi = lax.broadcasted_iota(jnp.int32, (R, R), 0)
        coli = lax.broadcasted_iota(jnp.int32, (R, R), 1)
        same = (rowi >> log2c) == (coli >> log2c)
        incl = same & (rowi >= coli)
        strict = same & (rowi > coli)
        neg_inf = -jnp.inf

        qkva, qkb = {}, {}
        for n in seqs:
            ya, tail_a = _causal_conv(qkva_ref[n], pa_sc[n], cwa_ref[...])
            yb, tail_b = _causal_conv(qkb_ref[n], pb_sc[n], cwb_ref[...])
            pa_sc[n] = tail_a
            pb_sc[n] = tail_b
            qkva[n] = jax.nn.silu(ya)
            qkb[n] = jax.nn.silu(yb)
        hd = HEADS * DK
        q_all = stack([qkva[n][:, h * DK:(h + 1) * DK] for n, h in blocks])
        k_all = stack([qkva[n][:, hd + h * DK:hd + (h + 1) * DK] for n, h in blocks])
        v_all = stack([qkva[n][:, 2 * hd + h * DV:2 * hd + (h + 1) * DV] for n, h in blocks])
        qb_all = stack([qkb[n][:, h * DK:(h + 1) * DK] for n, h in blocks])
        kb_all = stack([qkb[n][:, hd + h * DK:hd + (h + 1) * DK] for n, h in blocks]) * (DK ** -0.5)
        vb_all = stack([vb_ref[n, :, h * DV:(h + 1) * DV] for n, h in blocks])

        sm = stack([sm_ref[n] for n in seqs])
        lane = lax.broadcasted_iota(jnp.int32, (T, LANES), 1)
        g_full = -jnp.exp(prm_ref[1:2]) * jax.nn.softplus(sm + prm_ref[0:1])
        beta_full = jax.nn.sigmoid(sm)
        logi_full = sm + prm_ref[2:3]
        logf_full = jax.nn.log_sigmoid(sm + prm_ref[3:4])
        if npad:
            real = (lax.broadcasted_iota(jnp.int32, (T, LANES), 0) & (C - 1)) >= npad
            g_full = jnp.where(real, g_full, 0.0)
            beta_full = jnp.where(real, beta_full, 0.0)
            logf_full = jnp.where(real, logf_full, 0.0)
        cs_in = jnp.where(lane < L_A + HEADS, g_full,
                          jnp.where((lane >= L_F) & (lane < L_F + HEADS), logf_full, 0.0))
        ti = lax.broadcasted_iota(jnp.int32, (T, T), 0)
        tj = lax.broadcasted_iota(jnp.int32, (T, T), 1)
        tril = (((ti >> log2c) == (tj >> log2c)) & (ti >= tj)).astype(F32)
        cums = _mm_hi(tril, cs_in)
        yield
        gates = jnp.where((lane >= L_B) & (lane < L_B + HEADS), beta_full,
                          jnp.where((lane >= L_I) & (lane < L_I + HEADS), logi_full, cums))
        last = stack([jnp.broadcast_to(cums[k * C + C - 1:k * C + C, :], (C, LANES)) for k in range(GS)])

        def col(src, base):
            return stack([jnp.broadcast_to(src[(n - seqs[0]) * C:(n - seqs[0] + 1) * C, base + h:base + h + 1], (C, LANES))
                          for n, h in blocks])

        g_c, beta_c, li_c, f_c = col(gates, L_A), col(gates, L_B), col(gates, L_I), col(gates, L_F)
        glast_c, fend_c = col(last, L_A), col(last, L_F)
        lane_r = lax.broadcasted_iota(jnp.int32, (R, LANES), 1)
        rowform = jnp.where(lane_r == 0, g_c, jnp.where(lane_r == 1, li_c, f_c)).T
        g_r, li_r, f_r = rowform[0:1, :], rowform[1:2, :], rowform[2:3, :]
        if npad:
            li_c = jnp.where((lax.broadcasted_iota(jnp.int32, (R, LANES), 0) & (C - 1)) >= npad, li_c, neg_inf)
            li_r = jnp.where((coli[0:1, :] & (C - 1)) >= npad, li_r, neg_inf)

        q_all = q_all * lax.rsqrt(jnp.sum(q_all * q_all, axis=-1, keepdims=True) + RMS_EPS) * (DK ** -0.5)
        k_all = k_all * lax.rsqrt(jnp.sum(k_all * k_all, axis=-1, keepdims=True) + RMS_EPS)
        decay = jnp.exp(jnp.where(incl, wide(g_c) - g_r, neg_inf))
        low = jnp.where(strict, wide(beta_c) * decay * _mm_nt(k_all, k_all), 0.0)
        yield
        eye = (rowi == coli).astype(F32)
        xinv = eye - jnp.where(((rowi & 1) == 1) & (coli == rowi - 1), low, 0.0)
        shift = 1
        while (1 << shift) < C:
            br = rowi >> shift
            bc = coli >> shift
            cm = jnp.where(((br & 1) == 1) & (bc == br - 1), low, 0.0)
            t = _mm(xinv, cm)
            yield
            xinv = xinv - _mm(t, xinv)
            yield
            shift += 1
        eg = jnp.exp(g_c)
        u_all = _mm(xinv, beta_c * v_all)
        w_all = _mm(xinv, (beta_c * eg) * k_all)
        yield
        qk = _mm_nt(q_all, k_all) * decay
        yield
        q_dec = q_all * eg
        k_end = k_all * jnp.exp(glast_c - g_c)
        g_end = jnp.exp(glast_c)
        deltas, inter = [], []
        for b, (n, h) in enumerate(blocks):
            s = s_sc[n, h]
            delta = u_all[rows(b)] - _mm_nt(w_all[rows(b)], s)
            inter.append(_mm_nt(q_dec[rows(b)], s))
            s_sc[n, h] = g_end[b * C:b * C + 1, :] * s + _mm_tn(delta, k_end[rows(b)])
            deltas.append(delta)
            yield
        o_all = stack(inter) + _mm(qk, stack(deltas))
        yield
        za_all = stack([za_ref[n, :, h * DV:(h + 1) * DV] for n, h in blocks])
        o_all = (_rms(o_all, gnorm_ref[...]) * jax.nn.silu(za_all)).astype(BF16)
        for b, (n, h) in enumerate(blocks):
            oa_ref[n, :, h * DV:(h + 1) * DV] = o_all[rows(b)]

        dmat = jnp.where(incl, wide(f_c) - f_r + li_r, neg_inf)
        dmax = jnp.broadcast_to(jnp.max(dmat, axis=1, keepdims=True), (R, LANES))
        qkm = _mm_nt(qb_all, kb_all)
        yield
        ms = m_sc[mrows, :]
        m_inter = f_c + ms
        m_t = jnp.maximum(m_inter, dmax)
        w_intra = jnp.exp(dmat - wide(m_t)) * qkm
        w_inter = jnp.exp(m_inter - m_t)
        e_col = fend_c - f_c + li_c
        e_max = jnp.max(e_col.reshape(NBLK, C, LANES), axis=1, keepdims=True)
        e_max = jnp.broadcast_to(e_max, (NBLK, C, LANES)).reshape(R, LANES)
        m_new = jnp.maximum(fend_c + ms, e_max)
        dec = jnp.exp(fend_c + ms - m_new)
        wk = jnp.exp(e_col - m_new)
        wkv = wk * vb_all
        wkk = wk * kb_all
        inter, ns_rows = [], []
        for b, (n, h) in enumerate(blocks):
            cs = c_sc[n, h]
            ns = n_sc[n, h:h + 1, :]
            inter.append(_mm_nt(qb_all[rows(b)], cs))
            ns_rows.append(jnp.broadcast_to(ns, (C, LANES)))
            dec_b = dec[b * C:b * C + 1, :]
            c_sc[n, h] = dec_b * cs + _mm_tn(wkv[rows(b)], kb_all[rows(b)])
            n_sc[n, h:h + 1, :] = dec_b * ns + jnp.sum(wkk[rows(b)], axis=0, keepdims=True)
            yield
        m_sc[mrows, :] = m_new
        yield
        num = w_inter * stack(inter) + _mm(w_intra, vb_all)
        den = (w_inter[:, 0:1] * jnp.sum(qb_all * stack(ns_rows), axis=1, keepdims=True)
               + jnp.sum(w_intra, axis=1, keepdims=True))
        hh = num / jnp.maximum(jnp.abs(den), jnp.exp(-m_t[:, 0:1]))
        mnorm = stack([jnp.broadcast_to(mnorm_ref[h:h + 1, :], (C, DV)) for _, h in blocks])
        og_all = stack([og_ref[n, :, h * DV:(h + 1) * DV] for n, h in blocks])
        hh = (_rms(hh, mnorm) * jax.nn.sigmoid(og_all)).astype(BF16)
        for b, (n, h) in enumerate(blocks):
            ob_ref[n, :, h * DV:(h + 1) * DV] = hh[rows(b)]

    gens = [group(grp) for grp in range(NB // GS)]
    while gens:
        for gen in list(gens):
            try:
                next(gen)
            except StopIteration:
                gens.remove(gen)

    @pl.when(ci == nc - 1)
    def _fin():
        so_ref[...] = s_sc[...]
        co_ref[...] = c_sc[...]
        no_ref[...] = n_sc[...]
        tao_ref[...] = pa_sc[...]
        tbo_ref[...] = pb_sc[...]
        for n in range(NB):
            for h in range(HEADS):
                b = n * HEADS + h
                mo_ref[n, h:h + 1, :] = m_sc[b * C:b * C + 1, :]


def _mixers(proj3, init, consts, C, npad, groups):
    B, L, _ = proj3.shape
    nc = L // C
    assert C & (C - 1) == 0 and (2 * LANES) % (HEADS * C) == 0, "chunk must stack into 2*LANES rows"
    NB = groups * ((2 * LANES) // (HEADS * C))
    assert B % NB == 0
    cwa, cwb, prm, gnorm, mnorm = consts
    wa = cwa.shape[1]
    wb = cwb.shape[1]
    tail_a, tail_b, s0, c0, n0, m0 = init
    shared_init = s0.shape[0] == 1 and B > 1
    assert shared_init or s0.shape[0] == B
    NBI = 1 if shared_init else NB
    wh = HEADS * DV

    def col(width, start):
        return pl.BlockSpec((NB, C, width), lambda b, c, _s=start // width: (b, c, _s))

    def st(shape):
        nd = len(shape)
        if shared_init:
            return pl.BlockSpec((1,) + shape, lambda b, c, _n=nd: (0,) * (_n + 1))
        return pl.BlockSpec((NB,) + shape, lambda b, c, _n=nd: (b,) + (0,) * _n)

    def full(a):
        return pl.BlockSpec(a.shape, lambda b, c, _n=a.ndim: (0,) * _n)

    def outst(shape):
        return pl.BlockSpec((NB,) + shape, lambda b, c, _n=len(shape): (b,) + (0,) * _n)

    in_specs = [col(wa, C_QKVA), col(wh, C_ZA), col(wb, C_QKB), col(wh, C_VB), col(wh, C_OB), col(LANES, C_SMALL),
                st((TAIL, wa)), st((TAIL, wb)), st((HEADS, DV, DK)), st((HEADS, DV, DK)), st((HEADS, DK)), st((HEADS, LANES)),
                full(cwa), full(cwb), full(prm), full(gnorm), full(mnorm)]
    out_specs = [pl.BlockSpec((NB, C, wh), lambda b, c: (b, c, 0)),
                 pl.BlockSpec((NB, C, wh), lambda b, c: (b, c, 0)),
                 outst((HEADS, DV, DK)), outst((TAIL, wa)), outst((HEADS, DV, DK)), outst((HEADS, DK)),
                 outst((HEADS, LANES)), outst((TAIL, wb))]
    out_shape = [jax.ShapeDtypeStruct((B, L, wh), BF16), jax.ShapeDtypeStruct((B, L, wh), BF16),
                 jax.ShapeDtypeStruct((B, HEADS, DV, DK), F32), jax.ShapeDtypeStruct((B, TAIL, wa), F32),
                 jax.ShapeDtypeStruct((B, HEADS, DV, DK), F32), jax.ShapeDtypeStruct((B, HEADS, DK), F32),
                 jax.ShapeDtypeStruct((B, HEADS, LANES), F32), jax.ShapeDtypeStruct((B, TAIL, wb), F32)]
    scratch = [pltpu.VMEM((NB, HEADS, DV, DK), F32), pltpu.VMEM((NB, HEADS, DV, DK), F32),
               pltpu.VMEM((NB, HEADS, DK), F32), pltpu.VMEM((NB * HEADS * C, LANES), F32),
               pltpu.VMEM((NB, TAIL, wa), F32), pltpu.VMEM((NB, TAIL, wb), F32)]
    return pl.pallas_call(
        functools.partial(_mixer_kernel, C, nc, npad, NB, NBI),
        grid=(B // NB, nc),
        in_specs=in_specs, out_specs=out_specs, out_shape=out_shape, scratch_shapes=scratch,
        compiler_params=pltpu.CompilerParams(dimension_semantics=("arbitrary", "arbitrary"),
                                             vmem_limit_bytes=VMEM_LIMIT),
        name="mixers",
    )(proj3, proj3, proj3, proj3, proj3, proj3, tail_a, tail_b, s0, c0, n0, m0, cwa, cwb, prm, gnorm, mnorm)


def _out_proj_kernel(oa_ref, ob_ref, ga_ref, gb_ref, h_ref, wba_ref, wbb_ref, wo_ref, o_ref):
    ya = jnp.dot(oa_ref[...], wba_ref[...], preferred_element_type=F32)
    yb = jnp.dot(ob_ref[...], wbb_ref[...], preferred_element_type=F32)
    merged = jax.nn.sigmoid(ga_ref[...]) * ya + jax.nn.sigmoid(gb_ref[...]) * yb
    o_ref[...] = h_ref[...] + jnp.dot(merged.astype(BF16), wo_ref[...], preferred_element_type=F32)


def _out_proj(oa, ob, proj, h2d, wba, wbb, wo, tm):
    rows = h2d.shape[0]
    w_half = HEADS * DV

    def rowblk(width, cidx=0):
        return pl.BlockSpec((tm, width), lambda i, _c=cidx: (i, _c))

    def full(a):
        return pl.BlockSpec(a.shape, lambda i: (0, 0))

    return pl.pallas_call(
        _out_proj_kernel,
        grid=(rows // tm,),
        in_specs=[rowblk(w_half), rowblk(w_half), rowblk(D_MODEL, C_GA // D_MODEL), rowblk(D_MODEL, C_GB // D_MODEL),
                  rowblk(D_MODEL), full(wba), full(wbb), full(wo)],
        out_specs=rowblk(D_MODEL),
        out_shape=jax.ShapeDtypeStruct((rows, D_MODEL), F32),
        compiler_params=pltpu.CompilerParams(dimension_semantics=("arbitrary",), vmem_limit_bytes=VMEM_LIMIT),
        name="out_proj",
    )(oa, ob, proj, proj, h2d, wba, wbb, wo)


N_RANK = PEER_TOPK + 1
_STAIR = [(k, N_RANK // (k + 1)) for k in range(N_RANK)]
_N_CAND = sum(n for _, n in _STAIR)
_CAND_ROWS = -(-_N_CAND // 8) * 8


def _sort16_network():
    pairs = []

    def merge(lo, n, r):
        step = r * 2
        if step < n:
            merge(lo, n, step)
            merge(lo + r, n, step)
            for i in range(lo + r, lo + n - r, step):
                pairs.append((i, i + r))
        else:
            pairs.append((lo, lo + r))

    def sort(lo, n):
        if n > 1:
            m = n // 2
            sort(lo, m)
            sort(lo + m, m)
            merge(lo, n, 1)

    sort(0, 16)
    return pairs


_SORT16 = _sort16_network()


def _top_values(x):
    tiles = [x[8 * v:8 * v + 8, :] for v in range(N_KEYS // 8)]
    for a, b in _SORT16:
        hi = jnp.maximum(tiles[a], tiles[b])
        tiles[b] = jnp.minimum(tiles[a], tiles[b])
        tiles[a] = hi
    yield
    vals = []
    for kk in range(N_RANK):
        mx = jnp.max(tiles[0], axis=0, keepdims=True)
        vals.append(mx)
        hit = tiles[0] == mx
        depth = min(len(tiles) - 1, N_RANK - 1 - kk)
        for v in range(depth):
            tiles[v] = jnp.where(hit, tiles[v + 1], tiles[v])
        if depth == len(tiles) - 1:
            tiles[depth] = jnp.where(hit, -jnp.inf, tiles[depth])
        yield
    return vals


def _peer_select_steps(TB, hd, q_sc, keys_ref, gates):
    r1_sc, e1_sc, cnt_sc, e0_sc = gates
    neg_inf = -jnp.inf
    s_lists = [lax.dot_general(keys_ref[2 * hd + p], q_sc[2 * hd + p], (((1,), (1,)), ((), ())),
                               preferred_element_type=F32) for p in range(2)]
    yield
    rank1_t, e1_t, cnt_t, e0_t = [], [], [], []
    for lt in range(TB // LANES):
        tok = slice(lt * LANES, (lt + 1) * LANES)
        s0 = s_lists[0][:, tok]
        s1 = s_lists[1][:, tok]
        top1 = yield from _top_values(s1)
        top0 = yield from _top_values(s0)
        rank1 = jnp.full((N_KEYS, LANES), float(N_KEYS - 1), F32)
        for kk in reversed(range(N_RANK)):
            rank1 = jnp.where(s1 >= top1[kk], float(kk), rank1)
            if kk % 4 == 0:
                yield
        top1_arr = jnp.concatenate(top1, axis=0)
        cand = jnp.concatenate([top0[kk] + top1_arr[0:n] for kk, n in _STAIR]
                               + [jnp.full((_CAND_ROWS - _N_CAND, LANES), neg_inf, F32)], axis=0)
        top = top0[0] + top1[0]
        z = jnp.zeros((1, LANES), F32)
        tau = top
        x = cand
        for kk in range(PEER_TOPK):
            tau = jnp.max(x, axis=0, keepdims=True)
            z = z + jnp.exp(tau - top)
            x = jnp.where(x == tau, neg_inf, x)
            if kk % 4 == 3:
                yield
        tau = 0.5 * (tau + jnp.max(x, axis=0, keepdims=True))
        thr = tau - s0
        cnt = jnp.zeros((N_KEYS, LANES), F32)
        for kk in range(PEER_TOPK):
            cnt = jnp.where(top1[kk] >= thr, float(kk + 1), cnt)
            if kk % 4 == 3:
                yield
        rank1_t.append(rank1.astype(BF16))
        e1_t.append(jnp.exp(s1 - top1[0]).astype(BF16))
        cnt_t.append(cnt)
        e0_t.append(jnp.exp(s0 - top0[0]) / z)
        yield
    r1_sc[hd] = jnp.concatenate(rank1_t, axis=1)
    e1_sc[hd] = jnp.concatenate(e1_t, axis=1)
    cnt_sc[hd] = jnp.concatenate(cnt_t, axis=1)
    e0_sc[hd] = jnp.concatenate(e0_t, axis=1)


def _peer_kernel(TB, NI, NBLK, *refs):
    (h_ref, hn_ref, gf_ref, wq_ref, keys_ref, u0_ref, ua_ref, ub_ref, vt_ref, gfin_ref, o_ref,
     xn0_sc, xn1_sc, q_sc, r1a_sc, e1a_sc, cnta_sc, e0a_sc, r1b_sc, e1b_sc, cntb_sc, e0b_sc,
     acc_sc, act0_sc, act1_sc) = refs
    i = pl.program_id(0)
    g = pl.program_id(1)
    ng = pl.num_programs(1)
    EB = NI * N_KEYS
    sets = ((xn0_sc, (r1a_sc, e1a_sc, cnta_sc, e0a_sc)), (xn1_sc, (r1b_sc, e1b_sc, cntb_sc, e0b_sc)))
    act_bufs = (act0_sc, act1_sc)
    u_next = (ua_ref, ub_ref)

    def pre_act(u_blk, xn_sc):
        return lax.dot_general(u_blk, xn_sc[...], (((1,), (1,)), ((), ())), preferred_element_type=F32)

    def project(h, xn_sc):
        xn = _rms(h, gf_ref[...]).astype(BF16)
        xn_sc[...] = xn
        q = jnp.dot(xn, wq_ref[...], preferred_element_type=F32)
        for j in range(2 * PEER_HEADS):
            q_sc[j] = q[:, j * LANES:(j + 1) * LANES].astype(BF16)

    @pl.when((i == 0) & (g == 0))
    def _first_block():
        project(h_ref[...], xn0_sc)

        def head(hd, carry):
            for _ in _peer_select_steps(TB, hd, q_sc, keys_ref, sets[0][1]):
                pass
            return carry

        lax.fori_loop(0, PEER_HEADS, head, 0)

    def body(par):
        xn_cur, (r1_sc, e1_sc, cnt_sc, e0_sc) = sets[par]
        xn_nxt, gates_nxt = sets[1 - par]

        @pl.when(g == 0)
        def _start():
            acc_sc[...] = jnp.zeros_like(acc_sc)
            act0_sc[...] = pre_act(u0_ref[...], xn_cur)
            project(hn_ref[...], xn_nxt)

        def main_steps():
            for sub in range(2):
                cur, nxt = sub, 1 - sub
                act_bufs[nxt][...] = pre_act(u_next[sub][...], xn_cur)
                yield
                parts = []
                for ii in range(NI):
                    i_glob = (2 * g + sub) * NI + ii
                    act = act_bufs[cur][ii * N_KEYS:(ii + 1) * N_KEYS, :]
                    act = 0.5 * act * (1.0 + lax.erf(act * (2.0 ** -0.5)))
                    wgt = jnp.zeros((N_KEYS, TB), BF16)
                    for hd in range(PEER_HEADS):
                        cnt = cnt_sc[hd, pl.ds(i_glob, 1), :].astype(BF16)
                        e0 = e0_sc[hd, pl.ds(i_glob, 1), :].astype(BF16)
                        wgt = wgt + jnp.where(r1_sc[hd] < cnt, e1_sc[hd], jnp.zeros((), BF16)) * e0
                    parts.append(act.astype(BF16) * wgt)
                    yield
                acc_sc[...] += jnp.dot(vt_ref[:, sub * EB:(sub + 1) * EB], jnp.concatenate(parts, axis=0),
                                       preferred_element_type=F32)
                yield

        select = _peer_select_steps(TB, g, q_sc, keys_ref, gates_nxt)
        main = main_steps()
        next(select)
        main_live = select_live = True
        while main_live or select_live:
            if main_live:
                main_live = next(main, "done") != "done"
            for _ in range(SELECT_STEPS_PER_MAIN_STEP):
                if select_live:
                    select_live = next(select, "done") != "done"

        @pl.when(g == ng - 1)
        def _fin():
            o_ref[...] = _rms(h_ref[...] + acc_sc[...].T, gfin_ref[...])

    for par in range(2):
        pl.when(i % 2 == par)(functools.partial(body, par))


def _peer(h2d, gf, wq, keys, u_bf, vt_bf, gfin, tb, ni):
    rows = h2d.shape[0]
    n_exp = u_bf.shape[0]
    eb = ni * N_KEYS
    nblk = n_exp // eb
    ntb = rows // tb
    assert nblk == 2 * PEER_HEADS, "two expert blocks and one selection head per grid step"
    once = pl.Buffered(1)
    gate_bf = pltpu.VMEM((PEER_HEADS, N_KEYS, tb), BF16)
    gate_f32 = pltpu.VMEM((PEER_HEADS, N_KEYS, tb), F32)
    return pl.pallas_call(
        functools.partial(_peer_kernel, tb, ni, nblk),
        grid=(ntb, nblk // 2),
        in_specs=[pl.BlockSpec((tb, D_MODEL), lambda i, e: (i, 0)),
                  pl.BlockSpec((tb, D_MODEL), lambda i, e: (jnp.minimum(i + 1, ntb - 1), 0), pipeline_mode=once),
                  pl.BlockSpec((1, D_MODEL), lambda i, e: (0, 0)),
                  pl.BlockSpec(wq.shape, lambda i, e: (0, 0), pipeline_mode=once),
                  pl.BlockSpec(keys.shape, lambda i, e: (0, 0, 0), pipeline_mode=once),
                  pl.BlockSpec((eb, D_MODEL), lambda i, e: (0, 0), pipeline_mode=once),
                  pl.BlockSpec((eb, D_MODEL), lambda i, e: (jnp.minimum(2 * e + 1, nblk - 1), 0)),
                  pl.BlockSpec((eb, D_MODEL), lambda i, e: (jnp.minimum(2 * e + 2, nblk - 1), 0)),
                  pl.BlockSpec((D_MODEL, 2 * eb), lambda i, e: (0, e)),
                  pl.BlockSpec((1, D_MODEL), lambda i, e: (0, 0))],
        out_specs=pl.BlockSpec((tb, D_MODEL), lambda i, e: (i, 0)),
        out_shape=jax.ShapeDtypeStruct((rows, D_MODEL), F32),
        scratch_shapes=[pltpu.VMEM((tb, D_MODEL), BF16), pltpu.VMEM((tb, D_MODEL), BF16),
                        pltpu.VMEM((2 * PEER_HEADS, tb, LANES), BF16),
                        gate_bf, gate_bf, gate_f32, gate_f32, gate_bf, gate_bf, gate_f32, gate_f32,
                        pltpu.VMEM((D_MODEL, tb), F32),
                        pltpu.VMEM((eb, tb), F32), pltpu.VMEM((eb, tb), F32)],
        compiler_params=pltpu.CompilerParams(dimension_semantics=("arbitrary", "arbitrary"),
                                             vmem_limit_bytes=PEER_VMEM_LIMIT),
        name="peer",
    )(h2d, h2d, gf, wq, keys, u_bf, u_bf, u_bf, vt_bf, gfin)


def _tile(rows, cap):
    t = cap
    while t > 8 and rows % t:
        t //= 2
    return t if rows % t == 0 else rows


def _mixer_groups(batch, chunk):
    per_group = (2 * LANES) // (HEADS * chunk)
    groups = MIX_MAX_GROUPS
    while groups > 1 and (batch % (groups * per_group) or groups * per_group > MIX_MAX_SEQS):
        groups //= 2
    return groups


def _stream(x3, init, chunk, npad, wts, run_ffn):
    (norm_mix, w_proj, consts, wba, wbb, wo, norm_ffn, wq, keys, u_bf, vt_bf, norm_final) = wts
    B, L, _ = x3.shape
    x2 = x3.reshape(B * L, D_MODEL)
    proj = _norm_proj(x2, norm_mix, w_proj, _tile(B * L, 1024))
    outs = _mixers(proj.reshape(B, L, N_PROJ), init, consts, chunk, npad, _mixer_groups(B, chunk))
    oa, ob, s_new, tail_a, c_new, n_new, m_new, tail_b = outs
    states = (tail_a, tail_b, s_new, c_new, n_new, m_new)
    if not run_ffn:
        return None, states
    h2 = _out_proj(oa.reshape(B * L, -1), ob.reshape(B * L, -1), proj, x2, wba, wbb, wo, _tile(B * L, 512))
    y = _peer(h2, norm_ffn, wq, keys, u_bf, vt_bf, norm_final, _tile(B * L, 512), 8)
    return y.reshape(B, L, D_MODEL), states


def kernel(x_prompt, x_sample, state_gdn_S, state_gdn_conv, state_mlstm_C, state_mlstm_n, state_mlstm_m, state_mlstm_conv, meta_tokens, norm_mix, w_in, conv_gdn, gdn_a_log, gdn_dt_bias, gdn_out_norm, conv_mlstm, mlstm_i_bias, mlstm_f_bias, mlstm_out_norm, w_branch, w_out, norm_ffn, peer_w_query, peer_sub_keys, peer_u, peer_v, norm_final):
    assert w_in.shape[0] == 1, "one trunk layer"
    w = w_in[0]
    w_proj = jnp.concatenate(
        [w[:, 0:1536], w[:, 1544:2056], w[:, 2056:3080], w[:, 3080:3592], w[:, 3600:4112], w[:, 4112:6160],
         w[:, 1536:1544], w[:, 3592:3600], jnp.zeros((D_MODEL, N_PROJ - C_SMALL - 4 * HEADS), w.dtype)],
        axis=1).astype(BF16)

    def lane_row(vec, start):
        return jnp.zeros((LANES,), F32).at[start:start + HEADS].set(vec.astype(F32))

    prm = jnp.zeros((8, LANES), F32)
    prm = prm.at[0].set(lane_row(gdn_dt_bias[0], L_A)).at[1].set(lane_row(gdn_a_log[0], L_A))
    prm = prm.at[2].set(lane_row(mlstm_i_bias[0], L_I)).at[3].set(lane_row(mlstm_f_bias[0], L_F))
    consts = (conv_gdn[0], conv_mlstm[0], prm, gdn_out_norm[0].reshape(1, DV), mlstm_out_norm[0])
    wb = w_branch[0].astype(BF16)
    wts = (norm_mix[0].reshape(1, D_MODEL), w_proj, consts, wb[:HEADS * DV], wb[HEADS * DV:], w_out[0].astype(BF16),
           norm_ffn[0].reshape(1, D_MODEL), peer_w_query[0].astype(BF16),
           peer_sub_keys[0].reshape(2 * PEER_HEADS, N_KEYS, LANES).astype(BF16),
           peer_u[0].astype(BF16), peer_v[0].astype(BF16).T, norm_final.reshape(1, D_MODEL))

    def pad_tail(t):
        return jnp.pad(t, ((0, 0), (TAIL - (CONV_W - 1), 0), (0, 0)))

    def lanes(m):
        return jnp.broadcast_to(m[..., None], m.shape + (LANES,))

    wa = conv_gdn.shape[-1]
    wbw = conv_mlstm.shape[-1]
    zero_init = (jnp.zeros((1, TAIL, wa), F32), jnp.zeros((1, TAIL, wbw), F32),
                 jnp.zeros((1, HEADS, DV, DK), F32), jnp.zeros((1, HEADS, DV, DK), F32),
                 jnp.zeros((1, HEADS, DK), F32), jnp.zeros((1, HEADS, LANES), F32))
    x_meta = jnp.concatenate([jnp.zeros((CHUNK - N_META, D_MODEL), F32), meta_tokens.astype(F32)], axis=0)[None]
    _, meta_states = _stream(x_meta, zero_init, CHUNK, CHUNK - N_META, wts, False)
    y_p, st_p = _stream(x_prompt, meta_states, CHUNK, 0, wts, True)
    samp_init = (pad_tail(state_gdn_conv[0]), pad_tail(state_mlstm_conv[0]), state_gdn_S[0], state_mlstm_C[0],
                 state_mlstm_n[0], lanes(state_mlstm_m[0]))
    y_s, st_s = _stream(x_sample, samp_init, x_sample.shape[1], 0, wts, True)

    def pack(st):
        tail_a, tail_b, s_new, c_new, n_new, m_new = st
        return (s_new[None], tail_a[:, TAIL - (CONV_W - 1):][None], c_new[None], n_new[None], m_new[:, :, 0][None],
                tail_b[:, TAIL - (CONV_W - 1):][None])

    return (y_p, y_s) + pack(st_p) + pack(st_s)
```

```python
import functools

import jax
import jax.numpy as jnp
from jax import lax
from jax.experimental import pallas as pl
from jax.experimental.pallas import tpu as pltpu

F32 = jnp.float32
BF16 = jnp.bfloat16
HIGHEST = lax.Precision.HIGHEST

RMS_EPS = 1e-6
D_MODEL = 1024
N_META = 16
CHUNK = 64
HEADS = 4
DK = 128
DV = 128
PEER_HEADS = 8
N_KEYS = 128
PEER_TOPK = 16
LANES = 128
TAIL = 8
CONV_W = 4
VMEM_LIMIT = 56 * 1024 * 1024
MIX_MAX_GROUPS = 4
MIX_MAX_SEQS = 16
PEER_VMEM_LIMIT = 60 * 1024 * 1024
SELECT_STEPS_PER_MAIN_STEP = 12

C_QKVA, C_ZA, C_QKB, C_VB, C_OB, C_GA, C_GB, C_SMALL = 0, 1536, 2048, 3072, 3584, 4096, 5120, 6144
N_PROJ = 6272
L_A, L_B, L_I, L_F = 0, 4, 8, 12


def _rms(x, gain):
    return x * lax.rsqrt(jnp.mean(x * x, axis=-1, keepdims=True) + RMS_EPS) * gain


def _mm(a, b):
    return jnp.dot(a.astype(BF16), b.astype(BF16), preferred_element_type=F32)


def _mm_nt(a, b):
    return lax.dot_general(a.astype(BF16), b.astype(BF16), (((1,), (1,)), ((), ())), preferred_element_type=F32)


def _mm_tn(a, b):
    return lax.dot_general(a.astype(BF16), b.astype(BF16), (((0,), (0,)), ((), ())), preferred_element_type=F32)


def _mm_hi(a, b):
    return jnp.dot(a, b, preferred_element_type=F32, precision=HIGHEST)


def _norm_proj_kernel(x_ref, g_ref, w_ref, o_ref):
    o_ref[...] = jnp.dot(_rms(x_ref[...], g_ref[...]).astype(BF16), w_ref[...], preferred_element_type=F32)


def _norm_proj(x2d, gain, w_bf16, tm):
    rows = x2d.shape[0]
    return pl.pallas_call(
        _norm_proj_kernel,
        grid=(rows // tm,),
        in_specs=[pl.BlockSpec((tm, D_MODEL), lambda i: (i, 0)),
                  pl.BlockSpec((1, D_MODEL), lambda i: (0, 0)),
                  pl.BlockSpec((D_MODEL, N_PROJ), lambda i: (0, 0), pipeline_mode=pl.Buffered(1))],
        out_specs=pl.BlockSpec((tm, N_PROJ), lambda i: (i, 0)),
        out_shape=jax.ShapeDtypeStruct((rows, N_PROJ), F32),
        compiler_params=pltpu.CompilerParams(dimension_semantics=("arbitrary",), vmem_limit_bytes=VMEM_LIMIT),
        name="norm_proj",
    )(x2d, gain, w_bf16)


def _causal_conv(x, p8, w):
    xe = jnp.concatenate([p8, x], axis=0)
    y = x * w[CONV_W - 1:CONV_W]
    for k in range(1, CONV_W):
        y = y + pltpu.roll(xe, k, 0)[TAIL:] * w[CONV_W - 1 - k:CONV_W - k]
    return y, xe[x.shape[0]:]


def _mixer_kernel(C, nc, npad, NB, NBI,
                  qkva_ref, za_ref, qkb_ref, vb_ref, og_ref, sm_ref,
                  ta0_ref, tb0_ref, s0_ref, c0_ref, n0_ref, m0_ref,
                  cwa_ref, cwb_ref, prm_ref, gnorm_ref, mnorm_ref,
                  oa_ref, ob_ref, so_ref, tao_ref, co_ref, no_ref, mo_ref, tbo_ref,
                  s_sc, c_sc, n_sc, m_sc, pa_sc, pb_sc):
    ci = pl.program_id(1)
    GS = (2 * LANES) // (HEADS * C)
    NBLK = GS * HEADS
    R = NBLK * C
    T = GS * C
    log2c = C.bit_length() - 1

    def rows(b):
        return slice(b * C, (b + 1) * C)

    def stack(pieces):
        return jnp.concatenate(pieces, axis=0)

    def wide(x):
        return jnp.concatenate([x] * (R // LANES), axis=1)

    @pl.when(ci == 0)
    def _init():
        for n in range(NB):
            ni = min(n, NBI - 1)
            s_sc[n] = s0_ref[ni]
            c_sc[n] = c0_ref[ni]
            n_sc[n] = n0_ref[ni]
            pa_sc[n] = ta0_ref[ni]
            pb_sc[n] = tb0_ref[ni]
            for h in range(HEADS):
                m_sc[rows(n * HEADS + h), :] = jnp.broadcast_to(m0_ref[ni, h:h + 1, :], (C, LANES))

    def group(grp):
        seqs = list(range(grp * GS, (grp + 1) * GS))
        blocks = [(n, h) for n in seqs for h in range(HEADS)]
        mrows = slice(grp * R, (grp + 1) * R)
        rowi = lax.broadcasted_iota(jnp.int32, (R, R), 0)
        coli = lax.broadcasted_iota(jnp.int32, (R, R), 1)
        same = (rowi >> log2c) == (coli >> log2c)
        incl = same & (rowi >= coli)
        strict = same & (rowi > coli)
        neg_inf = -jnp.inf

        qkva, qkb = {}, {}
        for n in seqs:
            ya, tail_a = _causal_conv(qkva_ref[n], pa_sc[n], cwa_ref[...])
            yb, tail_b = _causal_conv(qkb_ref[n], pb_sc[n], cwb_ref[...])
            pa_sc[n] = tail_a
            pb_sc[n] = tail_b
            qkva[n] = jax.nn.silu(ya)
            qkb[n] = jax.nn.silu(yb)
        hd = HEADS * DK
        q_all = stack([qkva[n][:, h * DK:(h + 1) * DK] for n, h in blocks])
        k_all = stack([qkva[n][:, hd + h * DK:hd + (h + 1) * DK] for n, h in blocks])
        v_all = stack([qkva[n][:, 2 * hd + h * DV:2 * hd + (h + 1) * DV] for n, h in blocks])
        qb_all = stack([qkb[n][:, h * DK:(h + 1) * DK] for n, h in blocks])
        kb_all = stack([qkb[n][:, hd + h * DK:hd + (h + 1) * DK] for n, h in blocks]) * (DK ** -0.5)
        vb_all = stack([vb_ref[n, :, h * DV:(h + 1) * DV] for n, h in blocks])

        sm = stack([sm_ref[n] for n in seqs])
        lane = lax.broadcasted_iota(jnp.int32, (T, LANES), 1)
        g_full = -jnp.exp(prm_ref[1:2]) * jax.nn.softplus(sm + prm_ref[0:1])
        beta_full = jax.nn.sigmoid(sm)
        logi_full = sm + prm_ref[2:3]
        logf_full = jax.nn.log_sigmoid(sm + prm_ref[3:4])
        if npad:
            real = (lax.broadcasted_iota(jnp.int32, (T, LANES), 0) & (C - 1)) >= npad
            g_full = jnp.where(real, g_full, 0.0)
            beta_full = jnp.where(real, beta_full, 0.0)
            logf_full = jnp.where(real, logf_full, 0.0)
        cs_in = jnp.where(lane < L_A + HEADS, g_full,
                          jnp.where((lane >= L_F) & (lane < L_F + HEADS), logf_full, 0.0))
        ti = lax.broadcasted_iota(jnp.int32, (T, T), 0)
        tj = lax.broadcasted_iota(jnp.int32, (T, T), 1)
        tril = (((ti >> log2c) == (tj >> log2c)) & (ti >= tj)).astype(F32)
        cums = _mm_hi(tril, cs_in)
        yield
        gates = jnp.where((lane >= L_B) & (lane < L_B + HEADS), beta_full,
                          jnp.where((lane >= L_I) & (lane < L_I + HEADS), logi_full, cums))
        last = stack([jnp.broadcast_to(cums[k * C + C - 1:k * C + C, :], (C, LANES)) for k in range(GS)])

        def col(src, base):
            return stack([jnp.broadcast_to(src[(n - seqs[0]) * C:(n - seqs[0] + 1) * C, base + h:base + h + 1], (C, LANES))
                          for n, h in blocks])

        g_c, beta_c, li_c, f_c = col(gates, L_A), col(gates, L_B), col(gates, L_I), col(gates, L_F)
        glast_c, fend_c = col(last, L_A), col(last, L_F)
        lane_r = lax.broadcasted_iota(jnp.int32, (R, LANES), 1)
        rowform = jnp.where(lane_r == 0, g_c, jnp.where(lane_r == 1, li_c, f_c)).T
        g_r, li_r, f_r = rowform[0:1, :], rowform[1:2, :], rowform[2:3, :]
        if npad:
            li_c = jnp.where((lax.broadcasted_iota(jnp.int32, (R, LANES), 0) & (C - 1)) >= npad, li_c, neg_inf)
            li_r = jnp.where((coli[0:1, :] & (C - 1)) >= npad, li_r, neg_inf)

        q_all = q_all * lax.rsqrt(jnp.sum(q_all * q_all, axis=-1, keepdims=True) + RMS_EPS) * (DK ** -0.5)
        k_all = k_all * lax.rsqrt(jnp.sum(k_all * k_all, axis=-1, keepdims=True) + RMS_EPS)
        decay = jnp.exp(jnp.where(incl, wide(g_c) - g_r, neg_inf))
        low = jnp.where(strict, wide(beta_c) * decay * _mm_nt(k_all, k_all), 0.0)
        yield
        eye = (rowi == coli).astype(F32)
        xinv = eye - jnp.where(((rowi & 1) == 1) & (coli == rowi - 1), low, 0.0)
        shift = 1
        while (1 << shift) < C:
            br = rowi >> shift
            bc = coli >> shift
            cm = jnp.where(((br & 1) == 1) & (bc == br - 1), low, 0.0)
            t = _mm(xinv, cm)
            yield
            xinv = xinv - _mm(t, xinv)
            yield
            shift += 1
        eg = jnp.exp(g_c)
        u_all = _mm(xinv, beta_c * v_all)
        w_all = _mm(xinv, (beta_c * eg) * k_all)
        yield
        qk = _mm_nt(q_all, k_all) * decay
        yield
        q_dec = q_all * eg
        k_end = k_all * jnp.exp(glast_c - g_c)
        g_end = jnp.exp(glast_c)
        deltas, inter = [], []
        for b, (n, h) in enumerate(blocks):
            s = s_sc[n, h]
            delta = u_all[rows(b)] - _mm_nt(w_all[rows(b)], s)
            inter.append(_mm_nt(q_dec[rows(b)], s))
            s_sc[n, h] = g_end[b * C:b * C + 1, :] * s + _mm_tn(delta, k_end[rows(b)])
            deltas.append(delta)
            yield
        o_all = stack(inter) + _mm(qk, stack(deltas))
        yield
        za_all = stack([za_ref[n, :, h * DV:(h + 1) * DV] for n, h in blocks])
        o_all = (_rms(o_all, gnorm_ref[...]) * jax.nn.silu(za_all)).astype(BF16)
        for b, (n, h) in enumerate(blocks):
            oa_ref[n, :, h * DV:(h + 1) * DV] = o_all[rows(b)]

        dmat = jnp.where(incl, wide(f_c) - f_r + li_r, neg_inf)
        dmax = jnp.broadcast_to(jnp.max(dmat, axis=1, keepdims=True), (R, LANES))
        qkm = _mm_nt(qb_all, kb_all)
        yield
        ms = m_sc[mrows, :]
        m_inter = f_c + ms
        m_t = jnp.maximum(m_inter, dmax)
        w_intra = jnp.exp(dmat - wide(m_t)) * qkm
        w_inter = jnp.exp(m_inter - m_t)
        e_col = fend_c - f_c + li_c
        e_max = jnp.max(e_col.reshape(NBLK, C, LANES), axis=1, keepdims=True)
        e_max = jnp.broadcast_to(e_max, (NBLK, C, LANES)).reshape(R, LANES)
        m_new = jnp.maximum(fend_c + ms, e_max)
        dec = jnp.exp(fend_c + ms - m_new)
        wk = jnp.exp(e_col - m_new)
        wkv = wk * vb_all
        wkk = wk * kb_all
        inter, ns_rows = [], []
        for b, (n, h) in enumerate(blocks):
            cs = c_sc[n, h]
            ns = n_sc[n, h:h + 1, :]
            inter.append(_mm_nt(qb_all[rows(b)], cs))
            ns_rows.append(jnp.broadcast_to(ns, (C, LANES)))
            dec_b = dec[b * C:b * C + 1, :]
            c_sc[n, h] = dec_b * cs + _mm_tn(wkv[rows(b)], kb_all[rows(b)])
            n_sc[n, h:h + 1, :] = dec_b * ns + jnp.sum(wkk[rows(b)], axis=0, keepdims=True)
            yield
        m_sc[mrows, :] = m_new
        yield
        num = w_inter * stack(inter) + _mm(w_intra, vb_all)
        den = (w_inter[:, 0:1] * jnp.sum(qb_all * stack(ns_rows), axis=1, keepdims=True)
               + jnp.sum(w_intra, axis=1, keepdims=True))
        hh = num / jnp.maximum(jnp.abs(den), jnp.exp(-m_t[:, 0:1]))
        mnorm = stack([jnp.broadcast_to(mnorm_ref[h:h + 1, :], (C, DV)) for _, h in blocks])
        og_all = stack([og_ref[n, :, h * DV:(h + 1) * DV] for n, h in blocks])
        hh = (_rms(hh, mnorm) * jax.nn.sigmoid(og_all)).astype(BF16)
        for b, (n, h) in enumerate(blocks):
            ob_ref[n, :, h * DV:(h + 1) * DV] = hh[rows(b)]

    gens = [group(grp) for grp in range(NB // GS)]
    while gens:
        for gen in list(gens):
            try:
                next(gen)
            except StopIteration:
                gens.remove(gen)

    @pl.when(ci == nc - 1)
    def _fin():
        so_ref[...] = s_sc[...]
        co_ref[...] = c_sc[...]
        no_ref[...] = n_sc[...]
        tao_ref[...] = pa_sc[...]
        tbo_ref[...] = pb_sc[...]
        for n in range(NB):
            for h in range(HEADS):
                b = n * HEADS + h
                mo_ref[n, h:h + 1, :] = m_sc[b * C:b * C + 1, :]


def _mixers(proj3, init, consts, C, npad, groups):
    B, L, _ = proj3.shape
    nc = L // C
    assert C & (C - 1) == 0 and (2 * LANES) % (HEADS * C) == 0, "chunk must stack into 2*LANES rows"
    NB = groups * ((2 * LANES) // (HEADS * C))
    assert B % NB == 0
    cwa, cwb, prm, gnorm, mnorm = consts
    wa = cwa.shape[1]
    wb = cwb.shape[1]
    tail_a, tail_b, s0, c0, n0, m0 = init
    shared_init = s0.shape[0] == 1 and B > 1
    assert shared_init or s0.shape[0] == B
    NBI = 1 if shared_init else NB
    wh = HEADS * DV

    def col(width, start):
        return pl.BlockSpec((NB, C, width), lambda b, c, _s=start // width: (b, c, _s))

    def st(shape):
        nd = len(shape)
        if shared_init:
            return pl.BlockSpec((1,) + shape, lambda b, c, _n=nd: (0,) * (_n + 1))
        return pl.BlockSpec((NB,) + shape, lambda b, c, _n=nd: (b,) + (0,) * _n)

    def full(a):
        return pl.BlockSpec(a.shape, lambda b, c, _n=a.ndim: (0,) * _n)

    def outst(shape):
        return pl.BlockSpec((NB,) + shape, lambda b, c, _n=len(shape): (b,) + (0,) * _n)

    in_specs = [col(wa, C_QKVA), col(wh, C_ZA), col(wb, C_QKB), col(wh, C_VB), col(wh, C_OB), col(LANES, C_SMALL),
                st((TAIL, wa)), st((TAIL, wb)), st((HEADS, DV, DK)), st((HEADS, DV, DK)), st((HEADS, DK)), st((HEADS, LANES)),
                full(cwa), full(cwb), full(prm), full(gnorm), full(mnorm)]
    out_specs = [pl.BlockSpec((NB, C, wh), lambda b, c: (b, c, 0)),
                 pl.BlockSpec((NB, C, wh), lambda b, c: (b, c, 0)),
                 outst((HEADS, DV, DK)), outst((TAIL, wa)), outst((HEADS, DV, DK)), outst((HEADS, DK)),
                 outst((HEADS, LANES)), outst((TAIL, wb))]
    out_shape = [jax.ShapeDtypeStruct((B, L, wh), BF16), jax.ShapeDtypeStruct((B, L, wh), BF16),
                 jax.ShapeDtypeStruct((B, HEADS, DV, DK), F32), jax.ShapeDtypeStruct((B, TAIL, wa), F32),
                 jax.ShapeDtypeStruct((B, HEADS, DV, DK), F32), jax.ShapeDtypeStruct((B, HEADS, DK), F32),
                 jax.ShapeDtypeStruct((B, HEADS, LANES), F32), jax.ShapeDtypeStruct((B, TAIL, wb), F32)]
    scratch = [pltpu.VMEM((NB, HEADS, DV, DK), F32), pltpu.VMEM((NB, HEADS, DV, DK), F32),
               pltpu.VMEM((NB, HEADS, DK), F32), pltpu.VMEM((NB * HEADS * C, LANES), F32),
               pltpu.VMEM((NB, TAIL, wa), F32), pltpu.VMEM((NB, TAIL, wb), F32)]
    return pl.pallas_call(
        functools.partial(_mixer_kernel, C, nc, npad, NB, NBI),
        grid=(B // NB, nc),
        in_specs=in_specs, out_specs=out_specs, out_shape=out_shape, scratch_shapes=scratch,
        compiler_params=pltpu.CompilerParams(dimension_semantics=("arbitrary", "arbitrary"),
                                             vmem_limit_bytes=VMEM_LIMIT),
        name="mixers",
    )(proj3, proj3, proj3, proj3, proj3, proj3, tail_a, tail_b, s0, c0, n0, m0, cwa, cwb, prm, gnorm, mnorm)


def _out_proj_kernel(oa_ref, ob_ref, ga_ref, gb_ref, h_ref, wba_ref, wbb_ref, wo_ref, o_ref):
    ya = jnp.dot(oa_ref[...], wba_ref[...], preferred_element_type=F32)
    yb = jnp.dot(ob_ref[...], wbb_ref[...], preferred_element_type=F32)
    merged = jax.nn.sigmoid(ga_ref[...]) * ya + jax.nn.sigmoid(gb_ref[...]) * yb
    o_ref[...] = h_ref[...] + jnp.dot(merged.astype(BF16), wo_ref[...], preferred_element_type=F32)


def _out_proj(oa, ob, proj, h2d, wba, wbb, wo, tm):
    rows = h2d.shape[0]
    w_half = HEADS * DV

    def rowblk(width, cidx=0):
        return pl.BlockSpec((tm, width), lambda i, _c=cidx: (i, _c))

    def full(a):
        return pl.BlockSpec(a.shape, lambda i: (0, 0))

    return pl.pallas_call(
        _out_proj_kernel,
        grid=(rows // tm,),
        in_specs=[rowblk(w_half), rowblk(w_half), rowblk(D_MODEL, C_GA // D_MODEL), rowblk(D_MODEL, C_GB // D_MODEL),
                  rowblk(D_MODEL), full(wba), full(wbb), full(wo)],
        out_specs=rowblk(D_MODEL),
        out_shape=jax.ShapeDtypeStruct((rows, D_MODEL), F32),
        compiler_params=pltpu.CompilerParams(dimension_semantics=("arbitrary",), vmem_limit_bytes=VMEM_LIMIT),
        name="out_proj",
    )(oa, ob, proj, proj, h2d, wba, wbb, wo)


N_RANK = PEER_TOPK + 1
_STAIR = [(k, N_RANK // (k + 1)) for k in range(N_RANK)]
_N_CAND = sum(n for _, n in _STAIR)
_CAND_ROWS = -(-_N_CAND // 8) * 8


def _sort16_network():
    pairs = []

    def merge(lo, n, r):
        step = r * 2
        if step < n:
            merge(lo, n, step)
            merge(lo + r, n, step)
            for i in range(lo + r, lo + n - r, step):
                pairs.append((i, i + r))
        else:
            pairs.append((lo, lo + r))

    def sort(lo, n):
        if n > 1:
            m = n // 2
            sort(lo, m)
            sort(lo + m, m)
            merge(lo, n, 1)

    sort(0, 16)
    return pairs


_SORT16 = _sort16_network()


def _top_values(x):
    tiles = [x[8 * v:8 * v + 8, :] for v in range(N_KEYS // 8)]
    for a, b in _SORT16:
        hi = jnp.maximum(tiles[a], tiles[b])
        tiles[b] = jnp.minimum(tiles[a], tiles[b])
        tiles[a] = hi
    yield
    vals = []
    for kk in range(N_RANK):
        mx = jnp.max(tiles[0], axis=0, keepdims=True)
        vals.append(mx)
        hit = tiles[0] == mx
        depth = min(len(tiles) - 1, N_RANK - 1 - kk)
        for v in range(depth):
            tiles[v] = jnp.where(hit, tiles[v + 1], tiles[v])
        if depth == len(tiles) - 1:
            tiles[depth] = jnp.where(hit, -jnp.inf, tiles[depth])
        yield
    return vals


def _peer_select_steps(TB, hd, q_sc, keys_ref, gates):
    r1_sc, e1_sc, cnt_sc, e0_sc = gates
    neg_inf = -jnp.inf
    s_lists = [lax.dot_general(keys_ref[2 * hd + p], q_sc[2 * hd + p], (((1,), (1,)), ((), ())),
                               preferred_element_type=F32) for p in range(2)]
    yield
    rank1_t, e1_t, cnt_t, e0_t = [], [], [], []
    for lt in range(TB // LANES):
        tok = slice(lt * LANES, (lt + 1) * LANES)
        s0 = s_lists[0][:, tok]
        s1 = s_lists[1][:, tok]
        top1 = yield from _top_values(s1)
        top0 = yield from _top_values(s0)
        rank1 = jnp.full((N_KEYS, LANES), float(N_KEYS - 1), F32)
        for kk in reversed(range(N_RANK)):
            rank1 = jnp.where(s1 >= top1[kk], float(kk), rank1)
            if kk % 4 == 0:
                yield
        top1_arr = jnp.concatenate(top1, axis=0)
        cand = jnp.concatenate([top0[kk] + top1_arr[0:n] for kk, n in _STAIR]
                               + [jnp.full((_CAND_ROWS - _N_CAND, LANES), neg_inf, F32)], axis=0)
        top = top0[0] + top1[0]
        z = jnp.zeros((1, LANES), F32)
        tau = top
        x = cand
        for kk in range(PEER_TOPK):
            tau = jnp.max(x, axis=0, keepdims=True)
            z = z + jnp.exp(tau - top)
            x = jnp.where(x == tau, neg_inf, x)
            if kk % 4 == 3:
                yield
        tau = 0.5 * (tau + jnp.max(x, axis=0, keepdims=True))
        thr = tau - s0
        cnt = jnp.zeros((N_KEYS, LANES), F32)
        for kk in range(PEER_TOPK):
            cnt = jnp.where(top1[kk] >= thr, float(kk + 1), cnt)
            if kk % 4 == 3:
                yield
        rank1_t.append(rank1.astype(BF16))
        e1_t.append(jnp.exp(s1 - top1[0]).astype(BF16))
        cnt_t.append(cnt)
        e0_t.append(jnp.exp(s0 - top0[0]) / z)
        yield
    r1_sc[hd] = jnp.concatenate(rank1_t, axis=1)
    e1_sc[hd] = jnp.concatenate(e1_t, axis=1)
    cnt_sc[hd] = jnp.concatenate(cnt_t, axis=1)
    e0_sc[hd] = jnp.concatenate(e0_t, axis=1)


def _peer_kernel(TB, NI, NBLK, *refs):
    (h_ref, hn_ref, gf_ref, wq_ref, keys_ref, u0_ref, ua_ref, ub_ref, vt_ref, gfin_ref, o_ref,
     xn0_sc, xn1_sc, q_sc, r1a_sc, e1a_sc, cnta_sc, e0a_sc, r1b_sc, e1b_sc, cntb_sc, e0b_sc,
     acc_sc, act0_sc, act1_sc) = refs
    i = pl.program_id(0)
    g = pl.program_id(1)
    ng = pl.num_programs(1)
    EB = NI * N_KEYS
    sets = ((xn0_sc, (r1a_sc, e1a_sc, cnta_sc, e0a_sc)), (xn1_sc, (r1b_sc, e1b_sc, cntb_sc, e0b_sc)))
    act_bufs = (act0_sc, act1_sc)
    u_next = (ua_ref, ub_ref)

    def pre_act(u_blk, xn_sc):
        return lax.dot_general(u_blk, xn_sc[...], (((1,), (1,)), ((), ())), preferred_element_type=F32)

    def project(h, xn_sc):
        xn = _rms(h, gf_ref[...]).astype(BF16)
        xn_sc[...] = xn
        q = jnp.dot(xn, wq_ref[...], preferred_element_type=F32)
        for j in range(2 * PEER_HEADS):
            q_sc[j] = q[:, j * LANES:(j + 1) * LANES].astype(BF16)

    @pl.when((i == 0) & (g == 0))
    def _first_block():
        project(h_ref[...], xn0_sc)

        def head(hd, carry):
            for _ in _peer_select_steps(TB, hd, q_sc, keys_ref, sets[0][1]):
                pass
            return carry

        lax.fori_loop(0, PEER_HEADS, head, 0)

    def body(par):
        xn_cur, (r1_sc, e1_sc, cnt_sc, e0_sc) = sets[par]
        xn_nxt, gates_nxt = sets[1 - par]

        @pl.when(g == 0)
        def _start():
            acc_sc[...] = jnp.zeros_like(acc_sc)
            act0_sc[...] = pre_act(u0_ref[...], xn_cur)
            project(hn_ref[...], xn_nxt)

        def main_steps():
            for sub in range(2):
                cur, nxt = sub, 1 - sub
                act_bufs[nxt][...] = pre_act(u_next[sub][...], xn_cur)
                yield
                parts = []
                for ii in range(NI):
                    i_glob = (2 * g + sub) * NI + ii
                    act = act_bufs[cur][ii * N_KEYS:(ii + 1) * N_KEYS, :]
                    act = 0.5 * act * (1.0 + lax.erf(act * (2.0 ** -0.5)))
                    wgt = jnp.zeros((N_KEYS, TB), BF16)
                    for hd in range(PEER_HEADS):
                        cnt = cnt_sc[hd, pl.ds(i_glob, 1), :].astype(BF16)
                        e0 = e0_sc[hd, pl.ds(i_glob, 1), :].astype(BF16)
                        wgt = wgt + jnp.where(r1_sc[hd] < cnt, e1_sc[hd], jnp.zeros((), BF16)) * e0
                    parts.append(act.astype(BF16) * wgt)
                    yield
                acc_sc[...] += jnp.dot(vt_ref[:, sub * EB:(sub + 1) * EB], jnp.concatenate(parts, axis=0),
                                       preferred_element_type=F32)
                yield

        select = _peer_select_steps(TB, g, q_sc, keys_ref, gates_nxt)
        main = main_steps()
        next(select)
        main_live = select_live = True
        while main_live or select_live:
            if main_live:
                main_live = next(main, "done") != "done"
            for _ in range(SELECT_STEPS_PER_MAIN_STEP):
                if select_live:
                    select_live = next(select, "done") != "done"

        @pl.when(g == ng - 1)
        def _fin():
            o_ref[...] = _rms(h_ref[...] + acc_sc[...].T, gfin_ref[...])

    for par in range(2):
        pl.when(i % 2 == par)(functools.partial(body, par))


def _peer(h2d, gf, wq, keys, u_bf, vt_bf, gfin, tb, ni):
    rows = h2d.shape[0]
    n_exp = u_bf.shape[0]
    eb = ni * N_KEYS
    nblk = n_exp // eb
    ntb = rows // tb
    assert nblk == 2 * PEER_HEADS, "two expert blocks and one selection head per grid step"
    once = pl.Buffered(1)
    gate_bf = pltpu.VMEM((PEER_HEADS, N_KEYS, tb), BF16)
    gate_f32 = pltpu.VMEM((PEER_HEADS, N_KEYS, tb), F32)
    return pl.pallas_call(
        functools.partial(_peer_kernel, tb, ni, nblk),
        grid=(ntb, nblk // 2),
        in_specs=[pl.BlockSpec((tb, D_MODEL), lambda i, e: (i, 0)),
                  pl.BlockSpec((tb, D_MODEL), lambda i, e: (jnp.minimum(i + 1, ntb - 1), 0), pipeline_mode=once),
                  pl.BlockSpec((1, D_MODEL), lambda i, e: (0, 0)),
                  pl.BlockSpec(wq.shape, lambda i, e: (0, 0), pipeline_mode=once),
                  pl.BlockSpec(keys.shape, lambda i, e: (0, 0, 0), pipeline_mode=once),
                  pl.BlockSpec((eb, D_MODEL), lambda i, e: (0, 0), pipeline_mode=once),
                  pl.BlockSpec((eb, D_MODEL), lambda i, e: (jnp.minimum(2 * e + 1, nblk - 1), 0)),
                  pl.BlockSpec((eb, D_MODEL), lambda i, e: (jnp.minimum(2 * e + 2, nblk - 1), 0)),
                  pl.BlockSpec((D_MODEL, 2 * eb), lambda i, e: (0, e)),
                  pl.BlockSpec((1, D_MODEL), lambda i, e: (0, 0))],
        out_specs=pl.BlockSpec((tb, D_MODEL), lambda i, e: (i, 0)),
        out_shape=jax.ShapeDtypeStruct((rows, D_MODEL), F32),
        scratch_shapes=[pltpu.VMEM((tb, D_MODEL), BF16), pltpu.VMEM((tb, D_MODEL), BF16),
                        pltpu.VMEM((2 * PEER_HEADS, tb, LANES), BF16),
                        gate_bf, gate_bf, gate_f32, gate_f32, gate_bf, gate_bf, gate_f32, gate_f32,
                        pltpu.VMEM((D_MODEL, tb), F32),
                        pltpu.VMEM((eb, tb), F32), pltpu.VMEM((eb, tb), F32)],
        compiler_params=pltpu.CompilerParams(dimension_semantics=("arbitrary", "arbitrary"),
                                             vmem_limit_bytes=PEER_VMEM_LIMIT),
        name="peer",
    )(h2d, h2d, gf, wq, keys, u_bf, u_bf, u_bf, vt_bf, gfin)


def _tile(rows, cap):
    t = cap
    while t > 8 and rows % t:
        t //= 2
    return t if rows % t == 0 else rows


def _mixer_groups(batch, chunk):
    per_group = (2 * LANES) // (HEADS * chunk)
    groups = MIX_MAX_GROUPS
    while groups > 1 and (batch % (groups * per_group) or groups * per_group > MIX_MAX_SEQS):
        groups //= 2
    return groups


def _stream(x3, init, chunk, npad, wts, run_ffn):
    (norm_mix, w_proj, consts, wba, wbb, wo, norm_ffn, wq, keys, u_bf, vt_bf, norm_final) = wts
    B, L, _ = x3.shape
    x2 = x3.reshape(B * L, D_MODEL)
    proj = _norm_proj(x2, norm_mix, w_proj, _tile(B * L, 512))
    outs = _mixers(proj.reshape(B, L, N_PROJ), init, consts, chunk, npad, _mixer_groups(B, chunk))
    oa, ob, s_new, tail_a, c_new, n_new, m_new, tail_b = outs
    states = (tail_a, tail_b, s_new, c_new, n_new, m_new)
    if not run_ffn:
        return None, states
    h2 = _out_proj(oa.reshape(B * L, -1), ob.reshape(B * L, -1), proj, x2, wba, wbb, wo, _tile(B * L, 512))
    y = _peer(h2, norm_ffn, wq, keys, u_bf, vt_bf, norm_final, _tile(B * L, 512), 8)
    return y.reshape(B, L, D_MODEL), states


def kernel(x_prompt, x_sample, state_gdn_S, state_gdn_conv, state_mlstm_C, state_mlstm_n, state_mlstm_m, state_mlstm_conv, meta_tokens, norm_mix, w_in, conv_gdn, gdn_a_log, gdn_dt_bias, gdn_out_norm, conv_mlstm, mlstm_i_bias, mlstm_f_bias, mlstm_out_norm, w_branch, w_out, norm_ffn, peer_w_query, peer_sub_keys, peer_u, peer_v, norm_final):
    assert w_in.shape[0] == 1, "one trunk layer"
    w = w_in[0]
    w_proj = jnp.concatenate(
        [w[:, 0:1536], w[:, 1544:2056], w[:, 2056:3080], w[:, 3080:3592], w[:, 3600:4112], w[:, 4112:6160],
         w[:, 1536:1544], w[:, 3592:3600], jnp.zeros((D_MODEL, N_PROJ - C_SMALL - 4 * HEADS), w.dtype)],
        axis=1).astype(BF16)

    def lane_row(vec, start):
        return jnp.zeros((LANES,), F32).at[start:start + HEADS].set(vec.astype(F32))

    prm = jnp.zeros((8, LANES), F32)
    prm = prm.at[0].set(lane_row(gdn_dt_bias[0], L_A)).at[1].set(lane_row(gdn_a_log[0], L_A))
    prm = prm.at[2].set(lane_row(mlstm_i_bias[0], L_I)).at[3].set(lane_row(mlstm_f_bias[0], L_F))
    consts = (conv_gdn[0], conv_mlstm[0], prm, gdn_out_norm[0].reshape(1, DV), mlstm_out_norm[0])
    wb = w_branch[0].astype(BF16)
    wts = (norm_mix[0].reshape(1, D_MODEL), w_proj, consts, wb[:HEADS * DV], wb[HEADS * DV:], w_out[0].astype(BF16),
           norm_ffn[0].reshape(1, D_MODEL), peer_w_query[0].astype(BF16),
           peer_sub_keys[0].reshape(2 * PEER_HEADS, N_KEYS, LANES).astype(BF16),
           peer_u[0].astype(BF16), peer_v[0].T.astype(BF16), norm_final.reshape(1, D_MODEL))

    def pad_tail(t):
        return jnp.pad(t, ((0, 0), (TAIL - (CONV_W - 1), 0), (0, 0)))

    def lanes(m):
        return jnp.broadcast_to(m[..., None], m.shape + (LANES,))

    wa = conv_gdn.shape[-1]
    wbw = conv_mlstm.shape[-1]
    zero_init = (jnp.zeros((1, TAIL, wa), F32), jnp.zeros((1, TAIL, wbw), F32),
                 jnp.zeros((1, HEADS, DV, DK), F32), jnp.zeros((1, HEADS, DV, DK), F32),
                 jnp.zeros((1, HEADS, DK), F32), jnp.zeros((1, HEADS, LANES), F32))
    x_meta = jnp.concatenate([jnp.zeros((CHUNK - N_META, D_MODEL), F32), meta_tokens.astype(F32)], axis=0)[None]
    _, meta_states = _stream(x_meta, zero_init, CHUNK, CHUNK - N_META, wts, False)
    y_p, st_p = _stream(x_prompt, meta_states, CHUNK, 0, wts, True)
    samp_init = (pad_tail(state_gdn_conv[0]), pad_tail(state_mlstm_conv[0]), state_gdn_S[0], state_mlstm_C[0],
                 state_mlstm_n[0], lanes(state_mlstm_m[0]))
    y_s, st_s = _stream(x_sample, samp_init, x_sample.shape[1], 0, wts, True)

    def pack(st):
        tail_a, tail_b, s_new, c_new, n_new, m_new = st
        return (s_new[None], tail_a[:, TAIL - (CONV_W - 1):][None], c_new[None], n_new[None], m_new[:, :, 0][None],
                tail_b[:, TAIL - (CONV_W - 1):][None])

    return (y_p, y_s) + pack(st_p) + pack(st_s)
```

```python
import functools

import jax
import jax.numpy as jnp
from jax import lax
from jax.experimental import pallas as pl
from jax.experimental.pallas import tpu as pltpu

F32 = jnp.float32
BF16 = jnp.bfloat16
HIGHEST = lax.Precision.HIGHEST

RMS_EPS = 1e-6
D_MODEL = 1024
N_META = 16
CHUNK = 64
HEADS = 4
DK = 128
DV = 128
PEER_HEADS = 8
N_KEYS = 128
PEER_TOPK = 16
LANES = 128
TAIL = 8
CONV_W = 4
VMEM_LIMIT = 56 * 1024 * 1024
MIX_MAX_GROUPS = 4
MIX_MAX_SEQS = 16
PEER_VMEM_LIMIT = 60 * 1024 * 1024
SELECT_STEPS_PER_MAIN_STEP = 12

C_QKVA, C_ZA, C_QKB, C_VB, C_OB, C_GA, C_GB, C_SMALL = 0, 1536, 2048, 3072, 3584, 4096, 5120, 6144
N_PROJ = 6272
L_A, L_B, L_I, L_F = 0, 4, 8, 12


def _rms(x, gain):
    return x * lax.rsqrt(jnp.mean(x * x, axis=-1, keepdims=True) + RMS_EPS) * gain


def _mm(a, b):
    return jnp.dot(a.astype(BF16), b.astype(BF16), preferred_element_type=F32)


def _mm_nt(a, b):
    return lax.dot_general(a.astype(BF16), b.astype(BF16), (((1,), (1,)), ((), ())), preferred_element_type=F32)


def _mm_tn(a, b):
    return lax.dot_general(a.astype(BF16), b.astype(BF16), (((0,), (0,)), ((), ())), preferred_element_type=F32)


def _mm_hi(a, b):
    return jnp.dot(a, b, preferred_element_type=F32, precision=HIGHEST)


def _norm_proj_kernel(x_ref, g_ref, w_ref, o_ref):
    o_ref[...] = jnp.dot(_rms(x_ref[...], g_ref[...]).astype(BF16), w_ref[...], preferred_element_type=F32)


def _norm_proj(x2d, gain, w_bf16, tm):
    rows = x2d.shape[0]
    return pl.pallas_call(
        _norm_proj_kernel,
        grid=(rows // tm,),
        in_specs=[pl.BlockSpec((tm, D_MODEL), lambda i: (i, 0)),
                  pl.BlockSpec((1, D_MODEL), lambda i: (0, 0)),
                  pl.BlockSpec((D_MODEL, N_PROJ), lambda i: (0, 0), pipeline_mode=pl.Buffered(1))],
        out_specs=pl.BlockSpec((tm, N_PROJ), lambda i: (i, 0)),
        out_shape=jax.ShapeDtypeStruct((rows, N_PROJ), F32),
        compiler_params=pltpu.CompilerParams(dimension_semantics=("arbitrary",), vmem_limit_bytes=VMEM_LIMIT),
        name="norm_proj",
    )(x2d, gain, w_bf16)


def _causal_conv(x, p8, w):
    xe = jnp.concatenate([p8, x], axis=0)
    y = x * w[CONV_W - 1:CONV_W]
    for k in range(1, CONV_W):
        y = y + pltpu.roll(xe, k, 0)[TAIL:] * w[CONV_W - 1 - k:CONV_W - k]
    return y, xe[x.shape[0]:]


def _mixer_kernel(C, nc, npad, NB, NBI,
                  qkva_ref, za_ref, qkb_ref, vb_ref, og_ref, sm_ref,
                  ta0_ref, tb0_ref, s0_ref, c0_ref, n0_ref, m0_ref,
                  cwa_ref, cwb_ref, prm_ref, gnorm_ref, mnorm_ref,
                  ga_ref, gb_ref, h_ref, wba_ref, wbb_ref, wo_ref,
                  h2_ref, so_ref, tao_ref, co_ref, no_ref, mo_ref, tbo_ref,
                  s_sc, c_sc, n_sc, m_sc, pa_sc, pb_sc):
    ci = pl.program_id(1)
    branch_a, branch_b = {}, {}
    GS = (2 * LANES) // (HEADS * C)
    NBLK = GS * HEADS
    R = NBLK * C
    T = GS * C
    log2c = C.bit_length() - 1

    def rows(b):
        return slice(b * C, (b + 1) * C)

    def stack(pieces):
        return jnp.concatenate(pieces, axis=0)

    def wide(x):
        return jnp.concatenate([x] * (R // LANES), axis=1)

    @pl.when(ci == 0)
    def _init():
        for n in range(NB):
            ni = min(n, NBI - 1)
            s_sc[n] = s0_ref[ni]
            c_sc[n] = c0_ref[ni]
            n_sc[n] = n0_ref[ni]
            pa_sc[n] = ta0_ref[ni]
            pb_sc[n] = tb0_ref[ni]
            for h in range(HEADS):
                m_sc[rows(n * HEADS + h), :] = jnp.broadcast_to(m0_ref[ni, h:h + 1, :], (C, LANES))

    def group(grp):
        seqs = list(range(grp * GS, (grp + 1) * GS))
        blocks = [(n, h) for n in seqs for h in range(HEADS)]
        mrows = slice(grp * R, (grp + 1) * R)
        rowi = lax.broadcasted_iota(jnp.int32, (R, R), 0)
        coli = lax.broadcasted_iota(jnp.int32, (R, R), 1)
        same = (rowi >> log2c) == (coli >> log2c)
        incl = same & (rowi >= coli)
        strict = same & (rowi > coli)
        neg_inf = -jnp.inf

        qkva, qkb = {}, {}
        for n in seqs:
            ya, tail_a = _causal_conv(qkva_ref[n], pa_sc[n], cwa_ref[...])
            yb, tail_b = _causal_conv(qkb_ref[n], pb_sc[n], cwb_ref[...])
            pa_sc[n] = tail_a
            pb_sc[n] = tail_b
            qkva[n] = jax.nn.silu(ya)
            qkb[n] = jax.nn.silu(yb)
        hd = HEADS * DK
        q_all = stack([qkva[n][:, h * DK:(h + 1) * DK] for n, h in blocks])
        k_all = stack([qkva[n][:, hd + h * DK:hd + (h + 1) * DK] for n, h in blocks])
        v_all = stack([qkva[n][:, 2 * hd + h * DV:2 * hd + (h + 1) * DV] for n, h in blocks])
        qb_all = stack([qkb[n][:, h * DK:(h + 1) * DK] for n, h in blocks])
        kb_all = stack([qkb[n][:, hd + h * DK:hd + (h + 1) * DK] for n, h in blocks]) * (DK ** -0.5)
        vb_all = stack([vb_ref[n, :, h * DV:(h + 1) * DV] for n, h in blocks])

        sm = stack([sm_ref[n] for n in seqs])
        lane = lax.broadcasted_iota(jnp.int32, (T, LANES), 1)
        g_full = -jnp.exp(prm_ref[1:2]) * jax.nn.softplus(sm + prm_ref[0:1])
        beta_full = jax.nn.sigmoid(sm)
        logi_full = sm + prm_ref[2:3]
        logf_full = jax.nn.log_sigmoid(sm + prm_ref[3:4])
        if npad:
            real = (lax.broadcasted_iota(jnp.int32, (T, LANES), 0) & (C - 1)) >= npad
            g_full = jnp.where(real, g_full, 0.0)
            beta_full = jnp.where(real, beta_full, 0.0)
            logf_full = jnp.where(real, logf_full, 0.0)
        cs_in = jnp.where(lane < L_A + HEADS, g_full,
                          jnp.where((lane >= L_F) & (lane < L_F + HEADS), logf_full, 0.0))
        ti = lax.broadcasted_iota(jnp.int32, (T, T), 0)
        tj = lax.broadcasted_iota(jnp.int32, (T, T), 1)
        tril = (((ti >> log2c) == (tj >> log2c)) & (ti >= tj)).astype(F32)
        cums = _mm_hi(tril, cs_in)
        yield
        gates = jnp.where((lane >= L_B) & (lane < L_B + HEADS), beta_full,
                          jnp.where((lane >= L_I) & (lane < L_I + HEADS), logi_full, cums))
        last = stack([jnp.broadcast_to(cums[k * C + C - 1:k * C + C, :], (C, LANES)) for k in range(GS)])

        def col(src, base):
            return stack([jnp.broadcast_to(src[(n - seqs[0]) * C:(n - seqs[0] + 1) * C, base + h:base + h + 1], (C, LANES))
                          for n, h in blocks])

        g_c, beta_c, li_c, f_c = col(gates, L_A), col(gates, L_B), col(gates, L_I), col(gates, L_F)
        glast_c, fend_c = col(last, L_A), col(last, L_F)
        lane_r = lax.broadcasted_iota(jnp.int32, (R, LANES), 1)
        rowform = jnp.where(lane_r == 0, g_c, jnp.where(lane_r == 1, li_c, f_c)).T
        g_r, li_r, f_r = rowform[0:1, :], rowform[1:2, :], rowform[2:3, :]
        if npad:
            li_c = jnp.where((lax.broadcasted_iota(jnp.int32, (R, LANES), 0) & (C - 1)) >= npad, li_c, neg_inf)
            li_r = jnp.where((coli[0:1, :] & (C - 1)) >= npad, li_r, neg_inf)

        q_all = q_all * lax.rsqrt(jnp.sum(q_all * q_all, axis=-1, keepdims=True) + RMS_EPS) * (DK ** -0.5)
        k_all = k_all * lax.rsqrt(jnp.sum(k_all * k_all, axis=-1, keepdims=True) + RMS_EPS)
        decay = jnp.exp(jnp.where(incl, wide(g_c) - g_r, neg_inf))
        low = jnp.where(strict, wide(beta_c) * decay * _mm_nt(k_all, k_all), 0.0)
        yield
        eye = (rowi == coli).astype(F32)
        xinv = eye - jnp.where(((rowi & 1) == 1) & (coli == rowi - 1), low, 0.0)
        shift = 1
        while (1 << shift) < C:
            br = rowi >> shift
            bc = coli >> shift
            cm = jnp.where(((br & 1) == 1) & (bc == br - 1), low, 0.0)
            t = _mm(xinv, cm)
            yield
            xinv = xinv - _mm(t, xinv)
            yield
            shift += 1
        eg = jnp.exp(g_c)
        u_all = _mm(xinv, beta_c * v_all)
        w_all = _mm(xinv, (beta_c * eg) * k_all)
        yield
        qk = _mm_nt(q_all, k_all) * decay
        yield
        q_dec = q_all * eg
        k_end = k_all * jnp.exp(glast_c - g_c)
        g_end = jnp.exp(glast_c)
        deltas, inter = [], []
        for b, (n, h) in enumerate(blocks):
            s = s_sc[n, h]
            delta = u_all[rows(b)] - _mm_nt(w_all[rows(b)], s)
            inter.append(_mm_nt(q_dec[rows(b)], s))
            s_sc[n, h] = g_end[b * C:b * C + 1, :] * s + _mm_tn(delta, k_end[rows(b)])
            deltas.append(delta)
            yield
        o_all = stack(inter) + _mm(qk, stack(deltas))
        yield
        za_all = stack([za_ref[n, :, h * DV:(h + 1) * DV] for n, h in blocks])
        o_all = _rms(o_all, gnorm_ref[...]) * jax.nn.silu(za_all)
        for b, (n, h) in enumerate(blocks):
            branch_a[n, h] = o_all[rows(b)]

        dmat = jnp.where(incl, wide(f_c) - f_r + li_r, neg_inf)
        dmax = jnp.broadcast_to(jnp.max(dmat, axis=1, keepdims=True), (R, LANES))
        qkm = _mm_nt(qb_all, kb_all)
        yield
        ms = m_sc[mrows, :]
        m_inter = f_c + ms
        m_t = jnp.maximum(m_inter, dmax)
        w_intra = jnp.exp(dmat - wide(m_t)) * qkm
        w_inter = jnp.exp(m_inter - m_t)
        e_col = fend_c - f_c + li_c
        e_max = jnp.max(e_col.reshape(NBLK, C, LANES), axis=1, keepdims=True)
        e_max = jnp.broadcast_to(e_max, (NBLK, C, LANES)).reshape(R, LANES)
        m_new = jnp.maximum(fend_c + ms, e_max)
        dec = jnp.exp(fend_c + ms - m_new)
        wk = jnp.exp(e_col - m_new)
        wkv = wk * vb_all
        wkk = wk * kb_all
        inter, ns_rows = [], []
        for b, (n, h) in enumerate(blocks):
            cs = c_sc[n, h]
            ns = n_sc[n, h:h + 1, :]
            inter.append(_mm_nt(qb_all[rows(b)], cs))
            ns_rows.append(jnp.broadcast_to(ns, (C, LANES)))
            dec_b = dec[b * C:b * C + 1, :]
            c_sc[n, h] = dec_b * cs + _mm_tn(wkv[rows(b)], kb_all[rows(b)])
            n_sc[n, h:h + 1, :] = dec_b * ns + jnp.sum(wkk[rows(b)], axis=0, keepdims=True)
            yield
        m_sc[mrows, :] = m_new
        yield
        num = w_inter * stack(inter) + _mm(w_intra, vb_all)
        den = (w_inter[:, 0:1] * jnp.sum(qb_all * stack(ns_rows), axis=1, keepdims=True)
               + jnp.sum(w_intra, axis=1, keepdims=True))
        hh = num / jnp.maximum(jnp.abs(den), jnp.exp(-m_t[:, 0:1]))
        mnorm = stack([jnp.broadcast_to(mnorm_ref[h:h + 1, :], (C, DV)) for _, h in blocks])
        og_all = stack([og_ref[n, :, h * DV:(h + 1) * DV] for n, h in blocks])
        hh = _rms(hh, mnorm) * jax.nn.sigmoid(og_all)
        for b, (n, h) in enumerate(blocks):
            branch_b[n, h] = hh[rows(b)]

    gens = [group(grp) for grp in range(NB // GS)]
    while gens:
        for gen in list(gens):
            try:
                next(gen)
            except StopIteration:
                gens.remove(gen)

    def seq_rows(parts):
        return stack([jnp.concatenate([parts[n, h] for h in range(HEADS)], axis=1) for n in range(NB)]).astype(BF16)

    ya = jnp.dot(seq_rows(branch_a), wba_ref[...], preferred_element_type=F32)
    yb = jnp.dot(seq_rows(branch_b), wbb_ref[...], preferred_element_type=F32)
    merged = (jax.nn.sigmoid(stack([ga_ref[n] for n in range(NB)])) * ya
              + jax.nn.sigmoid(stack([gb_ref[n] for n in range(NB)])) * yb)
    h2 = stack([h_ref[n] for n in range(NB)]) + jnp.dot(merged.astype(BF16), wo_ref[...], preferred_element_type=F32)
    for n in range(NB):
        h2_ref[n] = h2[n * C:(n + 1) * C]

    @pl.when(ci == nc - 1)
    def _fin():
        so_ref[...] = s_sc[...]
        co_ref[...] = c_sc[...]
        no_ref[...] = n_sc[...]
        tao_ref[...] = pa_sc[...]
        tbo_ref[...] = pb_sc[...]
        for n in range(NB):
            for h in range(HEADS):
                b = n * HEADS + h
                mo_ref[n, h:h + 1, :] = m_sc[b * C:b * C + 1, :]


def _mixers(proj3, h3, init, consts, out_w, C, npad, groups):
    B, L, _ = proj3.shape
    nc = L // C
    assert C & (C - 1) == 0 and (2 * LANES) % (HEADS * C) == 0, "chunk must stack into 2*LANES rows"
    NB = groups * ((2 * LANES) // (HEADS * C))
    assert B % NB == 0
    cwa, cwb, prm, gnorm, mnorm = consts
    wba, wbb, wo = out_w
    wa = cwa.shape[1]
    wb = cwb.shape[1]
    tail_a, tail_b, s0, c0, n0, m0 = init
    shared_init = s0.shape[0] == 1 and B > 1
    assert shared_init or s0.shape[0] == B
    NBI = 1 if shared_init else NB
    wh = HEADS * DV

    def col(width, start):
        return pl.BlockSpec((NB, C, width), lambda b, c, _s=start // width: (b, c, _s))

    def st(shape):
        nd = len(shape)
        if shared_init:
            return pl.BlockSpec((1,) + shape, lambda b, c, _n=nd: (0,) * (_n + 1))
        return pl.BlockSpec((NB,) + shape, lambda b, c, _n=nd: (b,) + (0,) * _n)

    def full(a):
        return pl.BlockSpec(a.shape, lambda b, c, _n=a.ndim: (0,) * _n)

    def outst(shape):
        return pl.BlockSpec((NB,) + shape, lambda b, c, _n=len(shape): (b,) + (0,) * _n)

    in_specs = [col(wa, C_QKVA), col(wh, C_ZA), col(wb, C_QKB), col(wh, C_VB), col(wh, C_OB), col(LANES, C_SMALL),
                st((TAIL, wa)), st((TAIL, wb)), st((HEADS, DV, DK)), st((HEADS, DV, DK)), st((HEADS, DK)), st((HEADS, LANES)),
                full(cwa), full(cwb), full(prm), full(gnorm), full(mnorm),
                col(D_MODEL, C_GA), col(D_MODEL, C_GB), pl.BlockSpec((NB, C, D_MODEL), lambda b, c: (b, c, 0)),
                full(wba), full(wbb), full(wo)]
    out_specs = [pl.BlockSpec((NB, C, D_MODEL), lambda b, c: (b, c, 0)),
                 outst((HEADS, DV, DK)), outst((TAIL, wa)), outst((HEADS, DV, DK)), outst((HEADS, DK)),
                 outst((HEADS, LANES)), outst((TAIL, wb))]
    out_shape = [jax.ShapeDtypeStruct((B, L, D_MODEL), F32),
                 jax.ShapeDtypeStruct((B, HEADS, DV, DK), F32), jax.ShapeDtypeStruct((B, TAIL, wa), F32),
                 jax.ShapeDtypeStruct((B, HEADS, DV, DK), F32), jax.ShapeDtypeStruct((B, HEADS, DK), F32),
                 jax.ShapeDtypeStruct((B, HEADS, LANES), F32), jax.ShapeDtypeStruct((B, TAIL, wb), F32)]
    scratch = [pltpu.VMEM((NB, HEADS, DV, DK), F32), pltpu.VMEM((NB, HEADS, DV, DK), F32),
               pltpu.VMEM((NB, HEADS, DK), F32), pltpu.VMEM((NB * HEADS * C, LANES), F32),
               pltpu.VMEM((NB, TAIL, wa), F32), pltpu.VMEM((NB, TAIL, wb), F32)]
    return pl.pallas_call(
        functools.partial(_mixer_kernel, C, nc, npad, NB, NBI),
        grid=(B // NB, nc),
        in_specs=in_specs, out_specs=out_specs, out_shape=out_shape, scratch_shapes=scratch,
        compiler_params=pltpu.CompilerParams(dimension_semantics=("arbitrary", "arbitrary"),
                                             vmem_limit_bytes=VMEM_LIMIT),
        name="mixers",
    )(proj3, proj3, proj3, proj3, proj3, proj3, tail_a, tail_b, s0, c0, n0, m0, cwa, cwb, prm, gnorm, mnorm,
      proj3, proj3, h3, wba, wbb, wo)


N_RANK = PEER_TOPK + 1
_STAIR = [(k, N_RANK // (k + 1)) for k in range(N_RANK)]
_N_CAND = sum(n for _, n in _STAIR)
_CAND_ROWS = -(-_N_CAND // 8) * 8


def _sort16_network():
    pairs = []

    def merge(lo, n, r):
        step = r * 2
        if step < n:
            merge(lo, n, step)
            merge(lo + r, n, step)
            for i in range(lo + r, lo + n - r, step):
                pairs.append((i, i + r))
        else:
            pairs.append((lo, lo + r))

    def sort(lo, n):
        if n > 1:
            m = n // 2
            sort(lo, m)
            sort(lo + m, m)
            merge(lo, n, 1)

    sort(0, 16)
    return pairs


_SORT16 = _sort16_network()


def _top_values(x):
    tiles = [x[8 * v:8 * v + 8, :] for v in range(N_KEYS // 8)]
    for a, b in _SORT16:
        hi = jnp.maximum(tiles[a], tiles[b])
        tiles[b] = jnp.minimum(tiles[a], tiles[b])
        tiles[a] = hi
    yield
    vals = []
    for kk in range(N_RANK):
        mx = jnp.max(tiles[0], axis=0, keepdims=True)
        vals.append(mx)
        hit = tiles[0] == mx
        depth = min(len(tiles) - 1, N_RANK - 1 - kk)
        for v in range(depth):
            tiles[v] = jnp.where(hit, tiles[v + 1], tiles[v])
        if depth == len(tiles) - 1:
            tiles[depth] = jnp.where(hit, -jnp.inf, tiles[depth])
        yield
    return vals


def _peer_select_steps(TB, hd, q_sc, keys_ref, gates):
    r1_sc, e1_sc, cnt_sc, e0_sc = gates
    neg_inf = -jnp.inf
    s_lists = [lax.dot_general(keys_ref[2 * hd + p], q_sc[2 * hd + p], (((1,), (1,)), ((), ())),
                               preferred_element_type=F32) for p in range(2)]
    yield
    rank1_t, e1_t, cnt_t, e0_t = [], [], [], []
    for lt in range(TB // LANES):
        tok = slice(lt * LANES, (lt + 1) * LANES)
        s0 = s_lists[0][:, tok]
        s1 = s_lists[1][:, tok]
        top1 = yield from _top_values(s1)
        top0 = yield from _top_values(s0)
        rank1 = jnp.full((N_KEYS, LANES), float(N_KEYS - 1), F32)
        for kk in reversed(range(N_RANK)):
            rank1 = jnp.where(s1 >= top1[kk], float(kk), rank1)
            if kk % 4 == 0:
                yield
        top1_arr = jnp.concatenate(top1, axis=0)
        cand = jnp.concatenate([top0[kk] + top1_arr[0:n] for kk, n in _STAIR]
                               + [jnp.full((_CAND_ROWS - _N_CAND, LANES), neg_inf, F32)], axis=0)
        top = top0[0] + top1[0]
        z = jnp.zeros((1, LANES), F32)
        tau = top
        x = cand
        for kk in range(PEER_TOPK):
            tau = jnp.max(x, axis=0, keepdims=True)
            z = z + jnp.exp(tau - top)
            x = jnp.where(x == tau, neg_inf, x)
            if kk % 4 == 3:
                yield
        tau = 0.5 * (tau + jnp.max(x, axis=0, keepdims=True))
        thr = tau - s0
        cnt = jnp.zeros((N_KEYS, LANES), F32)
        for kk in range(PEER_TOPK):
            cnt = jnp.where(top1[kk] >= thr, float(kk + 1), cnt)
            if kk % 4 == 3:
                yield
        rank1_t.append(rank1.astype(BF16))
        e1_t.append(jnp.exp(s1 - top1[0]).astype(BF16))
        cnt_t.append(cnt)
        e0_t.append(jnp.exp(s0 - top0[0]) / z)
        yield
    r1_sc[hd] = jnp.concatenate(rank1_t, axis=1)
    e1_sc[hd] = jnp.concatenate(e1_t, axis=1)
    cnt_sc[hd] = jnp.concatenate(cnt_t, axis=1)
    e0_sc[hd] = jnp.concatenate(e0_t, axis=1)


def _peer_kernel(TB, NI, NBLK, *refs):
    (h_ref, hn_ref, gf_ref, wq_ref, keys_ref, u0_ref, ua_ref, ub_ref, vt_ref, gfin_ref, o_ref,
     xn0_sc, xn1_sc, q_sc, r1a_sc, e1a_sc, cnta_sc, e0a_sc, r1b_sc, e1b_sc, cntb_sc, e0b_sc,
     acc_sc, act0_sc, act1_sc) = refs
    i = pl.program_id(0)
    g = pl.program_id(1)
    ng = pl.num_programs(1)
    EB = NI * N_KEYS
    sets = ((xn0_sc, (r1a_sc, e1a_sc, cnta_sc, e0a_sc)), (xn1_sc, (r1b_sc, e1b_sc, cntb_sc, e0b_sc)))
    act_bufs = (act0_sc, act1_sc)
    u_next = (ua_ref, ub_ref)

    def pre_act(u_blk, xn_sc):
        return lax.dot_general(u_blk, xn_sc[...], (((1,), (1,)), ((), ())), preferred_element_type=F32)

    def project(h, xn_sc):
        xn = _rms(h, gf_ref[...]).astype(BF16)
        xn_sc[...] = xn
        q = jnp.dot(xn, wq_ref[...], preferred_element_type=F32)
        for j in range(2 * PEER_HEADS):
            q_sc[j] = q[:, j * LANES:(j + 1) * LANES].astype(BF16)

    @pl.when((i == 0) & (g == 0))
    def _first_block():
        project(h_ref[...], xn0_sc)

        def head(hd, carry):
            for _ in _peer_select_steps(TB, hd, q_sc, keys_ref, sets[0][1]):
                pass
            return carry

        lax.fori_loop(0, PEER_HEADS, head, 0)

    def body(par):
        xn_cur, (r1_sc, e1_sc, cnt_sc, e0_sc) = sets[par]
        xn_nxt, gates_nxt = sets[1 - par]

        @pl.when(g == 0)
        def _start():
            acc_sc[...] = jnp.zeros_like(acc_sc)
            act0_sc[...] = pre_act(u0_ref[...], xn_cur)
            project(hn_ref[...], xn_nxt)

        def main_steps():
            for sub in range(2):
                cur, nxt = sub, 1 - sub
                act_bufs[nxt][...] = pre_act(u_next[sub][...], xn_cur)
                yield
                parts = []
                for ii in range(NI):
                    i_glob = (2 * g + sub) * NI + ii
                    act = act_bufs[cur][ii * N_KEYS:(ii + 1) * N_KEYS, :]
                    act = 0.5 * act * (1.0 + lax.erf(act * (2.0 ** -0.5)))
                    wgt = jnp.zeros((N_KEYS, TB), BF16)
                    for hd in range(PEER_HEADS):
                        cnt = cnt_sc[hd, pl.ds(i_glob, 1), :].astype(BF16)
                        e0 = e0_sc[hd, pl.ds(i_glob, 1), :].astype(BF16)
                        wgt = wgt + jnp.where(r1_sc[hd] < cnt, e1_sc[hd], jnp.zeros((), BF16)) * e0
                    parts.append(act.astype(BF16) * wgt)
                    yield
                acc_sc[...] += jnp.dot(vt_ref[:, sub * EB:(sub + 1) * EB], jnp.concatenate(parts, axis=0),
                                       preferred_element_type=F32)
                yield

        select = _peer_select_steps(TB, g, q_sc, keys_ref, gates_nxt)
        main = main_steps()
        next(select)
        main_live = select_live = True
        while main_live or select_live:
            if main_live:
                main_live = next(main, "done") != "done"
            for _ in range(SELECT_STEPS_PER_MAIN_STEP):
                if select_live:
                    select_live = next(select, "done") != "done"

        @pl.when(g == ng - 1)
        def _fin():
            o_ref[...] = _rms(h_ref[...] + acc_sc[...].T, gfin_ref[...])

    for par in range(2):
        pl.when(i % 2 == par)(functools.partial(body, par))


def _peer(h2d, gf, wq, keys, u_bf, vt_bf, gfin, tb, ni):
    rows = h2d.shape[0]
    n_exp = u_bf.shape[0]
    eb = ni * N_KEYS
    nblk = n_exp // eb
    ntb = rows // tb
    assert nblk == 2 * PEER_HEADS, "two expert blocks and one selection head per grid step"
    once = pl.Buffered(1)
    gate_bf = pltpu.VMEM((PEER_HEADS, N_KEYS, tb), BF16)
    gate_f32 = pltpu.VMEM((PEER_HEADS, N_KEYS, tb), F32)
    return pl.pallas_call(
        functools.partial(_peer_kernel, tb, ni, nblk),
        grid=(ntb, nblk // 2),
        in_specs=[pl.BlockSpec((tb, D_MODEL), lambda i, e: (i, 0)),
                  pl.BlockSpec((tb, D_MODEL), lambda i, e: (jnp.minimum(i + 1, ntb - 1), 0), pipeline_mode=once),
                  pl.BlockSpec((1, D_MODEL), lambda i, e: (0, 0)),
                  pl.BlockSpec(wq.shape, lambda i, e: (0, 0), pipeline_mode=once),
                  pl.BlockSpec(keys.shape, lambda i, e: (0, 0, 0), pipeline_mode=once),
                  pl.BlockSpec((eb, D_MODEL), lambda i, e: (0, 0), pipeline_mode=once),
                  pl.BlockSpec((eb, D_MODEL), lambda i, e: (jnp.minimum(2 * e + 1, nblk - 1), 0)),
                  pl.BlockSpec((eb, D_MODEL), lambda i, e: (jnp.minimum(2 * e + 2, nblk - 1), 0)),
                  pl.BlockSpec((D_MODEL, 2 * eb), lambda i, e: (0, e)),
                  pl.BlockSpec((1, D_MODEL), lambda i, e: (0, 0))],
        out_specs=pl.BlockSpec((tb, D_MODEL), lambda i, e: (i, 0)),
        out_shape=jax.ShapeDtypeStruct((rows, D_MODEL), F32),
        scratch_shapes=[pltpu.VMEM((tb, D_MODEL), BF16), pltpu.VMEM((tb, D_MODEL), BF16),
                        pltpu.VMEM((2 * PEER_HEADS, tb, LANES), BF16),
                        gate_bf, gate_bf, gate_f32, gate_f32, gate_bf, gate_bf, gate_f32, gate_f32,
                        pltpu.VMEM((D_MODEL, tb), F32),
                        pltpu.VMEM((eb, tb), F32), pltpu.VMEM((eb, tb), F32)],
        compiler_params=pltpu.CompilerParams(dimension_semantics=("arbitrary", "arbitrary"),
                                             vmem_limit_bytes=PEER_VMEM_LIMIT),
        name="peer",
    )(h2d, h2d, gf, wq, keys, u_bf, u_bf, u_bf, vt_bf, gfin)


def _tile(rows, cap):
    t = cap
    while t > 8 and rows % t:
        t //= 2
    return t if rows % t == 0 else rows


def _mixer_groups(batch, chunk):
    per_group = (2 * LANES) // (HEADS * chunk)
    groups = MIX_MAX_GROUPS
    while groups > 1 and (batch % (groups * per_group) or groups * per_group > MIX_MAX_SEQS):
        groups //= 2
    return groups


def _stream(x3, init, chunk, npad, wts, run_ffn):
    (norm_mix, w_proj, consts, wba, wbb, wo, norm_ffn, wq, keys, u_bf, vt_bf, norm_final) = wts
    B, L, _ = x3.shape
    x2 = x3.reshape(B * L, D_MODEL)
    proj = _norm_proj(x2, norm_mix, w_proj, _tile(B * L, 512))
    outs = _mixers(proj.reshape(B, L, N_PROJ), x3, init, consts, (wba, wbb, wo), chunk, npad, _mixer_groups(B, chunk))
    h2, s_new, tail_a, c_new, n_new, m_new, tail_b = outs
    states = (tail_a, tail_b, s_new, c_new, n_new, m_new)
    if not run_ffn:
        return None, states
    y = _peer(h2.reshape(B * L, D_MODEL), norm_ffn, wq, keys, u_bf, vt_bf, norm_final, _tile(B * L, 512), 8)
    return y.reshape(B, L, D_MODEL), states


def kernel(x_prompt, x_sample, state_gdn_S, state_gdn_conv, state_mlstm_C, state_mlstm_n, state_mlstm_m, state_mlstm_conv, meta_tokens, norm_mix, w_in, conv_gdn, gdn_a_log, gdn_dt_bias, gdn_out_norm, conv_mlstm, mlstm_i_bias, mlstm_f_bias, mlstm_out_norm, w_branch, w_out, norm_ffn, peer_w_query, peer_sub_keys, peer_u, peer_v, norm_final):
    assert w_in.shape[0] == 1, "one trunk layer"
    w = w_in[0]
    w_proj = jnp.concatenate(
        [w[:, 0:1536], w[:, 1544:2056], w[:, 2056:3080], w[:, 3080:3592], w[:, 3600:4112], w[:, 4112:6160],
         w[:, 1536:1544], w[:, 3592:3600], jnp.zeros((D_MODEL, N_PROJ - C_SMALL - 4 * HEADS), w.dtype)],
        axis=1).astype(BF16)

    def lane_row(vec, start):
        return jnp.zeros((LANES,), F32).at[start:start + HEADS].set(vec.astype(F32))

    prm = jnp.zeros((8, LANES), F32)
    prm = prm.at[0].set(lane_row(gdn_dt_bias[0], L_A)).at[1].set(lane_row(gdn_a_log[0], L_A))
    prm = prm.at[2].set(lane_row(mlstm_i_bias[0], L_I)).at[3].set(lane_row(mlstm_f_bias[0], L_F))
    consts = (conv_gdn[0], conv_mlstm[0], prm, gdn_out_norm[0].reshape(1, DV), mlstm_out_norm[0])
    wb = w_branch[0].astype(BF16)
    wts = (norm_mix[0].reshape(1, D_MODEL), w_proj, consts, wb[:HEADS * DV], wb[HEADS * DV:], w_out[0].astype(BF16),
           norm_ffn[0].reshape(1, D_MODEL), peer_w_query[0].astype(BF16),
           peer_sub_keys[0].reshape(2 * PEER_HEADS, N_KEYS, LANES).astype(BF16),
           peer_u[0].astype(BF16), peer_v[0].T.astype(BF16), norm_final.reshape(1, D_MODEL))

    def pad_tail(t):
        return jnp.pad(t, ((0, 0), (TAIL - (CONV_W - 1), 0), (0, 0)))

    def lanes(m):
        return jnp.broadcast_to(m[..., None], m.shape + (LANES,))

    wa = conv_gdn.shape[-1]
    wbw = conv_mlstm.shape[-1]
    zero_init = (jnp.zeros((1, TAIL, wa), F32), jnp.zeros((1, TAIL, wbw), F32),
                 jnp.zeros((1, HEADS, DV, DK), F32), jnp.zeros((1, HEADS, DV, DK), F32),
                 jnp.zeros((1, HEADS, DK), F32), jnp.zeros((1, HEADS, LANES), F32))
    x_meta = jnp.concatenate([jnp.zeros((CHUNK - N_META, D_MODEL), F32), meta_tokens.astype(F32)], axis=0)[None]
    _, meta_states = _stream(x_meta, zero_init, CHUNK, CHUNK - N_META, wts, False)
    y_p, st_p = _stream(x_prompt, meta_states, CHUNK, 0, wts, True)
    samp_init = (pad_tail(state_gdn_conv[0]), pad_tail(state_mlstm_conv[0]), state_gdn_S[0], state_mlstm_C[0],
                 state_mlstm_n[0], lanes(state_mlstm_m[0]))
    y_s, st_s = _stream(x_sample, samp_init, x_sample.shape[1], 0, wts, True)

    def pack(st):
        tail_a, tail_b, s_new, c_new, n_new, m_new = st
        return (s_new[None], tail_a[:, TAIL - (CONV_W - 1):][None], c_new[None], n_new[None], m_new[:, :, 0][None],
                tail_b[:, TAIL - (CONV_W - 1):][None])

    return (y_p, y_s) + pack(st_p) + pack(st_s)
```

```python
import functools

import jax
import jax.numpy as jnp
from jax import lax
from jax.experimental import pallas as pl
from jax.experimental.pallas import tpu as pltpu

F32 = jnp.float32
BF16 = jnp.bfloat16
HIGHEST = lax.Precision.HIGHEST

RMS_EPS = 1e-6
D_MODEL = 1024
N_META = 16
CHUNK = 64
HEADS = 4
DK = 128
DV = 128
PEER_HEADS = 8
N_KEYS = 128
PEER_TOPK = 16
LANES = 128
TAIL = 8
CONV_W = 4
VMEM_LIMIT = 56 * 1024 * 1024
MIX_MAX_GROUPS = 4
MIX_MAX_SEQS = 16
PEER_VMEM_LIMIT = 60 * 1024 * 1024
SELECT_STEPS_PER_MAIN_STEP = 12

C_QKVA, C_ZA, C_QKB, C_VB, C_OB, C_GA, C_GB, C_SMALL = 0, 1536, 2048, 3072, 3584, 4096, 5120, 6144
N_PROJ = 6272
L_A, L_B, L_I, L_F = 0, 4, 8, 12


def _rms(x, gain):
    return x * lax.rsqrt(jnp.mean(x * x, axis=-1, keepdims=True) + RMS_EPS) * gain


def _mm(a, b):
    return jnp.dot(a.astype(BF16), b.astype(BF16), preferred_element_type=F32)


def _mm_nt(a, b):
    return lax.dot_general(a.astype(BF16), b.astype(BF16), (((1,), (1,)), ((), ())), preferred_element_type=F32)


def _mm_tn(a, b):
    return lax.dot_general(a.astype(BF16), b.astype(BF16), (((0,), (0,)), ((), ())), preferred_element_type=F32)


def _mm_hi(a, b):
    return jnp.dot(a, b, preferred_element_type=F32, precision=HIGHEST)


def _norm_proj_kernel(x_ref, g_ref, w_ref, o_ref):
    o_ref[...] = jnp.dot(_rms(x_ref[...], g_ref[...]).astype(BF16), w_ref[...], preferred_element_type=F32)


def _norm_proj(x2d, gain, w_bf16, tm):
    rows = x2d.shape[0]
    return pl.pallas_call(
        _norm_proj_kernel,
        grid=(rows // tm,),
        in_specs=[pl.BlockSpec((tm, D_MODEL), lambda i: (i, 0)),
                  pl.BlockSpec((1, D_MODEL), lambda i: (0, 0)),
                  pl.BlockSpec((D_MODEL, N_PROJ), lambda i: (0, 0), pipeline_mode=pl.Buffered(1))],
        out_specs=pl.BlockSpec((tm, N_PROJ), lambda i: (i, 0)),
        out_shape=jax.ShapeDtypeStruct((rows, N_PROJ), F32),
        compiler_params=pltpu.CompilerParams(dimension_semantics=("arbitrary",), vmem_limit_bytes=VMEM_LIMIT),
        name="norm_proj",
    )(x2d, gain, w_bf16)


def _causal_conv(x, p8, w):
    xe = jnp.concatenate([p8, x], axis=0)
    y = x * w[CONV_W - 1:CONV_W]
    for k in range(1, CONV_W):
        y = y + pltpu.roll(xe, k, 0)[TAIL:] * w[CONV_W - 1 - k:CONV_W - k]
    return y, xe[x.shape[0]:]


def _mixer_kernel(C, nc, npad, NB, NBI,
                  qkva_ref, za_ref, qkb_ref, vb_ref, og_ref, sm_ref,
                  ta0_ref, tb0_ref, s0_ref, c0_ref, n0_ref, m0_ref,
                  cwa_ref, cwb_ref, prm_ref, gnorm_ref, mnorm_ref,
                  ga_ref, gb_ref, h_ref, wba_ref, wbb_ref, wo_ref,
                  h2_ref, so_ref, tao_ref, co_ref, no_ref, mo_ref, tbo_ref,
                  s_sc, c_sc, n_sc, m_sc, pa_sc, pb_sc):
    ci = pl.program_id(1)
    branch_a, branch_b = {}, {}
    GS = (2 * LANES) // (HEADS * C)
    NBLK = GS * HEADS
    R = NBLK * C
    T = GS * C
    log2c = C.bit_length() - 1

    def rows(b):
        return slice(b * C, (b + 1) * C)

    def stack(pieces):
        return jnp.concatenate(pieces, axis=0)

    def wide(x):
        return jnp.concatenate([x] * (R // LANES), axis=1)

    @pl.when(ci == 0)
    def _init():
        for n in range(NB):
            ni = min(n, NBI - 1)
            s_sc[n] = s0_ref[ni]
            c_sc[n] = c0_ref[ni]
            n_sc[n] = n0_ref[ni]
            pa_sc[n] = ta0_ref[ni]
            pb_sc[n] = tb0_ref[ni]
            for h in range(HEADS):
                m_sc[rows(n * HEADS + h), :] = jnp.broadcast_to(m0_ref[ni, h:h + 1, :], (C, LANES))

    def group(grp):
        seqs = list(range(grp * GS, (grp + 1) * GS))
        blocks = [(n, h) for n in seqs for h in range(HEADS)]
        mrows = slice(grp * R, (grp + 1) * R)
        rowi = lax.broadcasted_iota(jnp.int32, (R, R), 0)
        coli = lax.broadcasted_iota(jnp.int32, (R, R), 1)
        same = (rowi >> log2c) == (coli >> log2c)
        incl = same & (rowi >= coli)
        strict = same & (rowi > coli)
        neg_inf = -jnp.inf

        qkva, qkb = {}, {}
        for n in seqs:
            ya, tail_a = _causal_conv(qkva_ref[n], pa_sc[n], cwa_ref[...])
            yb, tail_b = _causal_conv(qkb_ref[n], pb_sc[n], cwb_ref[...])
            pa_sc[n] = tail_a
            pb_sc[n] = tail_b
            qkva[n] = jax.nn.silu(ya)
            qkb[n] = jax.nn.silu(yb)
        hd = HEADS * DK
        q_all = stack([qkva[n][:, h * DK:(h + 1) * DK] for n, h in blocks])
        k_all = stack([qkva[n][:, hd + h * DK:hd + (h + 1) * DK] for n, h in blocks])
        v_all = stack([qkva[n][:, 2 * hd + h * DV:2 * hd + (h + 1) * DV] for n, h in blocks])
        qb_all = stack([qkb[n][:, h * DK:(h + 1) * DK] for n, h in blocks])
        kb_all = stack([qkb[n][:, hd + h * DK:hd + (h + 1) * DK] for n, h in blocks]) * (DK ** -0.5)
        vb_all = stack([vb_ref[n, :, h * DV:(h + 1) * DV] for n, h in blocks])

        sm = stack([sm_ref[n] for n in seqs])
        lane = lax.broadcasted_iota(jnp.int32, (T, LANES), 1)
        g_full = -jnp.exp(prm_ref[1:2]) * jax.nn.softplus(sm + prm_ref[0:1])
        beta_full = jax.nn.sigmoid(sm)
        logi_full = sm + prm_ref[2:3]
        logf_full = jax.nn.log_sigmoid(sm + prm_ref[3:4])
        if npad:
            real = (lax.broadcasted_iota(jnp.int32, (T, LANES), 0) & (C - 1)) >= npad
            g_full = jnp.where(real, g_full, 0.0)
            beta_full = jnp.where(real, beta_full, 0.0)
            logf_full = jnp.where(real, logf_full, 0.0)
        cs_in = jnp.where(lane < L_A + HEADS, g_full,
                          jnp.where((lane >= L_F) & (lane < L_F + HEADS), logf_full, 0.0))
        ti = lax.broadcasted_iota(jnp.int32, (T, T), 0)
        tj = lax.broadcasted_iota(jnp.int32, (T, T), 1)
        tril = (((ti >> log2c) == (tj >> log2c)) & (ti >= tj)).astype(F32)
        cums = _mm_hi(tril, cs_in)
        yield
        gates = jnp.where((lane >= L_B) & (lane < L_B + HEADS), beta_full,
                          jnp.where((lane >= L_I) & (lane < L_I + HEADS), logi_full, cums))
        last = stack([jnp.broadcast_to(cums[k * C + C - 1:k * C + C, :], (C, LANES)) for k in range(GS)])

        def col(src, base):
            return stack([jnp.broadcast_to(src[(n - seqs[0]) * C:(n - seqs[0] + 1) * C, base + h:base + h + 1], (C, LANES))
                          for n, h in blocks])

        g_c, beta_c, li_c, f_c = col(gates, L_A), col(gates, L_B), col(gates, L_I), col(gates, L_F)
        glast_c, fend_c = col(last, L_A), col(last, L_F)
        lane_r = lax.broadcasted_iota(jnp.int32, (R, LANES), 1)
        rowform = jnp.where(lane_r == 0, g_c, jnp.where(lane_r == 1, li_c, f_c)).T
        g_r, li_r, f_r = rowform[0:1, :], rowform[1:2, :], rowform[2:3, :]
        if npad:
            li_c = jnp.where((lax.broadcasted_iota(jnp.int32, (R, LANES), 0) & (C - 1)) >= npad, li_c, neg_inf)
            li_r = jnp.where((coli[0:1, :] & (C - 1)) >= npad, li_r, neg_inf)

        q_all = q_all * lax.rsqrt(jnp.sum(q_all * q_all, axis=-1, keepdims=True) + RMS_EPS) * (DK ** -0.5)
        k_all = k_all * lax.rsqrt(jnp.sum(k_all * k_all, axis=-1, keepdims=True) + RMS_EPS)
        decay = jnp.exp(jnp.where(incl, wide(g_c) - g_r, neg_inf))
        low = jnp.where(strict, wide(beta_c) * decay * _mm_nt(k_all, k_all), 0.0)
        yield
        eye = (rowi == coli).astype(F32)
        xinv = eye - jnp.where(((rowi & 1) == 1) & (coli == rowi - 1), low, 0.0)
        shift = 1
        while (1 << shift) < C:
            br = rowi >> shift
            bc = coli >> shift
            cm = jnp.where(((br & 1) == 1) & (bc == br - 1), low, 0.0)
            t = _mm(xinv, cm)
            yield
            xinv = xinv - _mm(t, xinv)
            yield
            shift += 1
        eg = jnp.exp(g_c)
        u_all = _mm(xinv, beta_c * v_all)
        w_all = _mm(xinv, (beta_c * eg) * k_all)
        yield
        qk = _mm_nt(q_all, k_all) * decay
        yield
        q_dec = q_all * eg
        k_end = k_all * jnp.exp(glast_c - g_c)
        g_end = jnp.exp(glast_c)
        deltas, inter = [], []
        for b, (n, h) in enumerate(blocks):
            s = s_sc[n, h]
            delta = u_all[rows(b)] - _mm_nt(w_all[rows(b)], s)
            inter.append(_mm_nt(q_dec[rows(b)], s))
            s_sc[n, h] = g_end[b * C:b * C + 1, :] * s + _mm_tn(delta, k_end[rows(b)])
            deltas.append(delta)
            yield
        o_all = stack(inter) + _mm(qk, stack(deltas))
        yield
        za_all = stack([za_ref[n, :, h * DV:(h + 1) * DV] for n, h in blocks])
        o_all = _rms(o_all, gnorm_ref[...]) * jax.nn.silu(za_all)
        for b, (n, h) in enumerate(blocks):
            branch_a[n, h] = o_all[rows(b)]

        dmat = jnp.where(incl, wide(f_c) - f_r + li_r, neg_inf)
        dmax = jnp.broadcast_to(jnp.max(dmat, axis=1, keepdims=True), (R, LANES))
        qkm = _mm_nt(qb_all, kb_all)
        yield
        ms = m_sc[mrows, :]
        m_inter = f_c + ms
        m_t = jnp.maximum(m_inter, dmax)
        w_intra = jnp.exp(dmat - wide(m_t)) * qkm
        w_inter = jnp.exp(m_inter - m_t)
        e_col = fend_c - f_c + li_c
        e_max = jnp.max(e_col.reshape(NBLK, C, LANES), axis=1, keepdims=True)
        e_max = jnp.broadcast_to(e_max, (NBLK, C, LANES)).reshape(R, LANES)
        m_new = jnp.maximum(fend_c + ms, e_max)
        dec = jnp.exp(fend_c + ms - m_new)
        wk = jnp.exp(e_col - m_new)
        wkv = wk * vb_all
        wkk = wk * kb_all
        inter, ns_rows = [], []
        for b, (n, h) in enumerate(blocks):
            cs = c_sc[n, h]
            ns = n_sc[n, h:h + 1, :]
            inter.append(_mm_nt(qb_all[rows(b)], cs))
            ns_rows.append(jnp.broadcast_to(ns, (C, LANES)))
            dec_b = dec[b * C:b * C + 1, :]
            c_sc[n, h] = dec_b * cs + _mm_tn(wkv[rows(b)], kb_all[rows(b)])
            n_sc[n, h:h + 1, :] = dec_b * ns + jnp.sum(wkk[rows(b)], axis=0, keepdims=True)
            yield
        m_sc[mrows, :] = m_new
        yield
        num = w_inter * stack(inter) + _mm(w_intra, vb_all)
        den = (w_inter[:, 0:1] * jnp.sum(qb_all * stack(ns_rows), axis=1, keepdims=True)
               + jnp.sum(w_intra, axis=1, keepdims=True))
        hh = num / jnp.maximum(jnp.abs(den), jnp.exp(-m_t[:, 0:1]))
        mnorm = stack([jnp.broadcast_to(mnorm_ref[h:h + 1, :], (C, DV)) for _, h in blocks])
        og_all = stack([og_ref[n, :, h * DV:(h + 1) * DV] for n, h in blocks])
        hh = _rms(hh, mnorm) * jax.nn.sigmoid(og_all)
        for b, (n, h) in enumerate(blocks):
            branch_b[n, h] = hh[rows(b)]

    gens = [group(grp) for grp in range(NB // GS)]
    while gens:
        for gen in list(gens):
            try:
                next(gen)
            except StopIteration:
                gens.remove(gen)

    def seq_rows(parts):
        return stack([jnp.concatenate([parts[n, h] for h in range(HEADS)], axis=1) for n in range(NB)]).astype(BF16)

    ya = jnp.dot(seq_rows(branch_a), wba_ref[...], preferred_element_type=F32)
    yb = jnp.dot(seq_rows(branch_b), wbb_ref[...], preferred_element_type=F32)
    merged = (jax.nn.sigmoid(stack([ga_ref[n] for n in range(NB)])) * ya
              + jax.nn.sigmoid(stack([gb_ref[n] for n in range(NB)])) * yb)
    h2 = stack([h_ref[n] for n in range(NB)]) + jnp.dot(merged.astype(BF16), wo_ref[...], preferred_element_type=F32)
    for n in range(NB):
        h2_ref[n] = h2[n * C:(n + 1) * C]

    @pl.when(ci == nc - 1)
    def _fin():
        so_ref[...] = s_sc[...]
        co_ref[...] = c_sc[...]
        no_ref[...] = n_sc[...]
        tao_ref[...] = pa_sc[...]
        tbo_ref[...] = pb_sc[...]
        for n in range(NB):
            for h in range(HEADS):
                b = n * HEADS + h
                mo_ref[n, h:h + 1, :] = m_sc[b * C:b * C + 1, :]


def _mixers(proj3, h3, init, consts, out_w, C, npad, groups):
    B, L, _ = proj3.shape
    nc = L // C
    assert C & (C - 1) == 0 and (2 * LANES) % (HEADS * C) == 0, "chunk must stack into 2*LANES rows"
    NB = groups * ((2 * LANES) // (HEADS * C))
    assert B % NB == 0
    cwa, cwb, prm, gnorm, mnorm = consts
    wba, wbb, wo = out_w
    wa = cwa.shape[1]
    wb = cwb.shape[1]
    tail_a, tail_b, s0, c0, n0, m0 = init
    shared_init = s0.shape[0] == 1 and B > 1
    assert shared_init or s0.shape[0] == B
    NBI = 1 if shared_init else NB
    wh = HEADS * DV

    def col(width, start):
        return pl.BlockSpec((NB, C, width), lambda b, c, _s=start // width: (b, c, _s))

    def st(shape):
        nd = len(shape)
        if shared_init:
            return pl.BlockSpec((1,) + shape, lambda b, c, _n=nd: (0,) * (_n + 1))
        return pl.BlockSpec((NB,) + shape, lambda b, c, _n=nd: (b,) + (0,) * _n)

    def full(a):
        return pl.BlockSpec(a.shape, lambda b, c, _n=a.ndim: (0,) * _n)

    def outst(shape):
        return pl.BlockSpec((NB,) + shape, lambda b, c, _n=len(shape): (b,) + (0,) * _n)

    in_specs = [col(wa, C_QKVA), col(wh, C_ZA), col(wb, C_QKB), col(wh, C_VB), col(wh, C_OB), col(LANES, C_SMALL),
                st((TAIL, wa)), st((TAIL, wb)), st((HEADS, DV, DK)), st((HEADS, DV, DK)), st((HEADS, DK)), st((HEADS, LANES)),
                full(cwa), full(cwb), full(prm), full(gnorm), full(mnorm),
                col(D_MODEL, C_GA), col(D_MODEL, C_GB), pl.BlockSpec((NB, C, D_MODEL), lambda b, c: (b, c, 0)),
                full(wba), full(wbb), full(wo)]
    out_specs = [pl.BlockSpec((NB, C, D_MODEL), lambda b, c: (b, c, 0)),
                 outst((HEADS, DV, DK)), outst((TAIL, wa)), outst((HEADS, DV, DK)), outst((HEADS, DK)),
                 outst((HEADS, LANES)), outst((TAIL, wb))]
    out_shape = [jax.ShapeDtypeStruct((B, L, D_MODEL), F32),
                 jax.ShapeDtypeStruct((B, HEADS, DV, DK), F32), jax.ShapeDtypeStruct((B, TAIL, wa), F32),
                 jax.ShapeDtypeStruct((B, HEADS, DV, DK), F32), jax.ShapeDtypeStruct((B, HEADS, DK), F32),
                 jax.ShapeDtypeStruct((B, HEADS, LANES), F32), jax.ShapeDtypeStruct((B, TAIL, wb), F32)]
    scratch = [pltpu.VMEM((NB, HEADS, DV, DK), F32), pltpu.VMEM((NB, HEADS, DV, DK), F32),
               pltpu.VMEM((NB, HEADS, DK), F32), pltpu.VMEM((NB * HEADS * C, LANES), F32),
               pltpu.VMEM((NB, TAIL, wa), F32), pltpu.VMEM((NB, TAIL, wb), F32)]
    return pl.pallas_call(
        functools.partial(_mixer_kernel, C, nc, npad, NB, NBI),
        grid=(B // NB, nc),
        in_specs=in_specs, out_specs=out_specs, out_shape=out_shape, scratch_shapes=scratch,
        compiler_params=pltpu.CompilerParams(dimension_semantics=("arbitrary", "arbitrary"),
                                             vmem_limit_bytes=VMEM_LIMIT),
        name="mixers",
    )(proj3, proj3, proj3, proj3, proj3, proj3, tail_a, tail_b, s0, c0, n0, m0, cwa, cwb, prm, gnorm, mnorm,
      proj3, proj3, h3, wba, wbb, wo)


N_RANK = PEER_TOPK + 1
_STAIR = [(k, N_RANK // (k + 1)) for k in range(N_RANK)]
_N_CAND = sum(n for _, n in _STAIR)
_CAND_ROWS = -(-_N_CAND // 8) * 8


def _sort16_network():
    pairs = []

    def merge(lo, n, r):
        step = r * 2
        if step < n:
            merge(lo, n, step)
            merge(lo + r, n, step)
            for i in range(lo + r, lo + n - r, step):
                pairs.append((i, i + r))
        else:
            pairs.append((lo, lo + r))

    def sort(lo, n):
        if n > 1:
            m = n // 2
            sort(lo, m)
            sort(lo + m, m)
            merge(lo, n, 1)

    sort(0, 16)
    return pairs


_SORT16 = _sort16_network()


def _top_values(x):
    tiles = [x[8 * v:8 * v + 8, :] for v in range(N_KEYS // 8)]
    for a, b in _SORT16:
        hi = jnp.maximum(tiles[a], tiles[b])
        tiles[b] = jnp.minimum(tiles[a], tiles[b])
        tiles[a] = hi
    yield
    vals = []
    for kk in range(N_RANK):
        mx = jnp.max(tiles[0], axis=0, keepdims=True)
        vals.append(mx)
        hit = tiles[0] == mx
        depth = min(len(tiles) - 1, N_RANK - 1 - kk)
        for v in range(depth):
            tiles[v] = jnp.where(hit, tiles[v + 1], tiles[v])
        if depth == len(tiles) - 1:
            tiles[depth] = jnp.where(hit, -jnp.inf, tiles[depth])
        yield
    return vals


def _peer_select_steps(TB, hd, q_sc, keys_ref, gates):
    r1_sc, e1_sc, cnt_sc, e0_sc = gates
    neg_inf = -jnp.inf
    s_lists = [lax.dot_general(keys_ref[2 * hd + p], q_sc[2 * hd + p], (((1,), (1,)), ((), ())),
                               preferred_element_type=F32) for p in range(2)]
    yield
    rank1_t, e1_t, cnt_t, e0_t = [], [], [], []
    for lt in range(TB // LANES):
        tok = slice(lt * LANES, (lt + 1) * LANES)
        s0 = s_lists[0][:, tok]
        s1 = s_lists[1][:, tok]
        top1 = yield from _top_values(s1)
        top0 = yield from _top_values(s0)
        rank1 = jnp.full((N_KEYS, LANES), float(N_KEYS - 1), F32)
        for kk in reversed(range(N_RANK)):
            rank1 = jnp.where(s1 >= top1[kk], float(kk), rank1)
            if kk % 4 == 0:
                yield
        top1_arr = jnp.concatenate(top1, axis=0)
        cand = jnp.concatenate([top0[kk] + top1_arr[0:n] for kk, n in _STAIR]
                               + [jnp.full((_CAND_ROWS - _N_CAND, LANES), neg_inf, F32)], axis=0)
        top = top0[0] + top1[0]
        z = jnp.zeros((1, LANES), F32)
        tau = top
        x = cand
        for kk in range(PEER_TOPK):
            tau = jnp.max(x, axis=0, keepdims=True)
            z = z + jnp.exp(tau - top)
            x = jnp.where(x == tau, neg_inf, x)
            if kk % 4 == 3:
                yield
        tau = 0.5 * (tau + jnp.max(x, axis=0, keepdims=True))
        thr = tau - s0
        cnt = jnp.zeros((N_KEYS, LANES), F32)
        for kk in range(PEER_TOPK):
            cnt = jnp.where(top1[kk] >= thr, float(kk + 1), cnt)
            if kk % 4 == 3:
                yield
        rank1_t.append(rank1.astype(BF16))
        e1_t.append(jnp.exp(s1 - top1[0]).astype(BF16))
        cnt_t.append(cnt)
        e0_t.append(jnp.exp(s0 - top0[0]) / z)
        yield
    r1_sc[hd] = jnp.concatenate(rank1_t, axis=1)
    e1_sc[hd] = jnp.concatenate(e1_t, axis=1)
    cnt_sc[hd] = jnp.concatenate(cnt_t, axis=1)
    e0_sc[hd] = jnp.concatenate(e0_t, axis=1)


def _peer_kernel(TB, NI, NBLK, *refs):
    (h_ref, hn_ref, gf_ref, wq_ref, keys_ref, u0_ref, ua_ref, ub_ref, vt_ref, gfin_ref, o_ref,
     xn0_sc, xn1_sc, q_sc, r1a_sc, e1a_sc, cnta_sc, e0a_sc, r1b_sc, e1b_sc, cntb_sc, e0b_sc,
     acc_sc, act0_sc, act1_sc) = refs
    i = pl.program_id(0)
    g = pl.program_id(1)
    ng = pl.num_programs(1)
    EB = NI * N_KEYS
    sets = ((xn0_sc, (r1a_sc, e1a_sc, cnta_sc, e0a_sc)), (xn1_sc, (r1b_sc, e1b_sc, cntb_sc, e0b_sc)))
    act_bufs = (act0_sc, act1_sc)
    u_next = (ua_ref, ub_ref)

    def pre_act(u_blk, xn_sc):
        return lax.dot_general(u_blk, xn_sc[...], (((1,), (1,)), ((), ())), preferred_element_type=F32)

    def project(h, xn_sc):
        xn = _rms(h, gf_ref[...]).astype(BF16)
        xn_sc[...] = xn
        q = jnp.dot(xn, wq_ref[...], preferred_element_type=F32)
        for j in range(2 * PEER_HEADS):
            q_sc[j] = q[:, j * LANES:(j + 1) * LANES].astype(BF16)

    @pl.when((i == 0) & (g == 0))
    def _first_block():
        project(h_ref[...], xn0_sc)

        def head(hd, carry):
            for _ in _peer_select_steps(TB, hd, q_sc, keys_ref, sets[0][1]):
                pass
            return carry

        lax.fori_loop(0, PEER_HEADS, head, 0)

    def body(par):
        xn_cur, (r1_sc, e1_sc, cnt_sc, e0_sc) = sets[par]
        xn_nxt, gates_nxt = sets[1 - par]

        @pl.when(g == 0)
        def _start():
            acc_sc[...] = jnp.zeros_like(acc_sc)
            act0_sc[...] = pre_act(u0_ref[...], xn_cur)
            project(hn_ref[...], xn_nxt)

        def main_steps():
            for sub in range(2):
                cur, nxt = sub, 1 - sub
                act_bufs[nxt][...] = pre_act(u_next[sub][...], xn_cur)
                yield
                parts = []
                for ii in range(NI):
                    i_glob = (2 * g + sub) * NI + ii
                    act = act_bufs[cur][ii * N_KEYS:(ii + 1) * N_KEYS, :]
                    act = 0.5 * act * (1.0 + lax.erf(act * (2.0 ** -0.5)))
                    wgt = jnp.zeros((N_KEYS, TB), BF16)
                    for hd in range(PEER_HEADS):
                        cnt = cnt_sc[hd, pl.ds(i_glob, 1), :].astype(BF16)
                        e0 = e0_sc[hd, pl.ds(i_glob, 1), :].astype(BF16)
                        wgt = wgt + jnp.where(r1_sc[hd] < cnt, e1_sc[hd], jnp.zeros((), BF16)) * e0
                    parts.append(act.astype(BF16) * wgt)
                    yield
                acc_sc[...] += jnp.dot(vt_ref[:, sub * EB:(sub + 1) * EB], jnp.concatenate(parts, axis=0),
                                       preferred_element_type=F32)
                yield

        select = _peer_select_steps(TB, g, q_sc, keys_ref, gates_nxt)
        main = main_steps()
        next(select)
        main_live = select_live = True
        while main_live or select_live:
            if main_live:
                main_live = next(main, "done") != "done"
            for _ in range(SELECT_STEPS_PER_MAIN_STEP):
                if select_live:
                    select_live = next(select, "done") != "done"

        @pl.when(g == ng - 1)
        def _fin():
            o_ref[...] = _rms(h_ref[...] + acc_sc[...].T, gfin_ref[...])

    for par in range(2):
        pl.when(i % 2 == par)(functools.partial(body, par))


def _peer(h2d, gf, wq, keys, u_bf, vt_bf, gfin, tb, ni):
    rows = h2d.shape[0]
    n_exp = u_bf.shape[0]
    eb = ni * N_KEYS
    nblk = n_exp // eb
    ntb = rows // tb
    assert nblk == 2 * PEER_HEADS, "two expert blocks and one selection head per grid step"
    once = pl.Buffered(1)
    gate_bf = pltpu.VMEM((PEER_HEADS, N_KEYS, tb), BF16)
    gate_f32 = pltpu.VMEM((PEER_HEADS, N_KEYS, tb), F32)
    return pl.pallas_call(
        functools.partial(_peer_kernel, tb, ni, nblk),
        grid=(ntb, nblk // 2),
        in_specs=[pl.BlockSpec((tb, D_MODEL), lambda i, e: (i, 0)),
                  pl.BlockSpec((tb, D_MODEL), lambda i, e: (jnp.minimum(i + 1, ntb - 1), 0), pipeline_mode=once),
                  pl.BlockSpec((1, D_MODEL), lambda i, e: (0, 0)),
                  pl.BlockSpec(wq.shape, lambda i, e: (0, 0), pipeline_mode=once),
                  pl.BlockSpec(keys.shape, lambda i, e: (0, 0, 0), pipeline_mode=once),
                  pl.BlockSpec((eb, D_MODEL), lambda i, e: (0, 0), pipeline_mode=once),
                  pl.BlockSpec((eb, D_MODEL), lambda i, e: (jnp.minimum(2 * e + 1, nblk - 1), 0)),
                  pl.BlockSpec((eb, D_MODEL), lambda i, e: (jnp.minimum(2 * e + 2, nblk - 1), 0)),
                  pl.BlockSpec((D_MODEL, 2 * eb), lambda i, e: (0, e)),
                  pl.BlockSpec((1, D_MODEL), lambda i, e: (0, 0))],
        out_specs=pl.BlockSpec((tb, D_MODEL), lambda i, e: (i, 0)),
        out_shape=jax.ShapeDtypeStruct((rows, D_MODEL), F32),
        scratch_shapes=[pltpu.VMEM((tb, D_MODEL), BF16), pltpu.VMEM((tb, D_MODEL), BF16),
                        pltpu.VMEM((2 * PEER_HEADS, tb, LANES), BF16),
                        gate_bf, gate_bf, gate_f32, gate_f32, gate_bf, gate_bf, gate_f32, gate_f32,
                        pltpu.VMEM((D_MODEL, tb), F32),
                        pltpu.VMEM((eb, tb), F32), pltpu.VMEM((eb, tb), F32)],
        compiler_params=pltpu.CompilerParams(dimension_semantics=("arbitrary", "arbitrary"),
                                             vmem_limit_bytes=PEER_VMEM_LIMIT),
        name="peer",
    )(h2d, h2d, gf, wq, keys, u_bf, u_bf, u_bf, vt_bf, gfin)


def _cast_transpose_kernel(x_ref, o_ref):
    o_ref[...] = x_ref[...].T.astype(BF16)


def _cast_transpose(x, tr):
    n, d = x.shape
    return pl.pallas_call(
        _cast_transpose_kernel,
        grid=(n // tr,),
        in_specs=[pl.BlockSpec((tr, d), lambda i: (i, 0))],
        out_specs=pl.BlockSpec((d, tr), lambda i: (0, i)),
        out_shape=jax.ShapeDtypeStruct((d, n), BF16),
        compiler_params=pltpu.CompilerParams(dimension_semantics=("arbitrary",), vmem_limit_bytes=VMEM_LIMIT),
        name="cast_transpose",
    )(x)


def _tile(rows, cap):
    t = cap
    while t > 8 and rows % t:
        t //= 2
    return t if rows % t == 0 else rows


def _mixer_groups(batch, chunk):
    per_group = (2 * LANES) // (HEADS * chunk)
    groups = MIX_MAX_GROUPS
    while groups > 1 and (batch % (groups * per_group) or groups * per_group > MIX_MAX_SEQS):
        groups //= 2
    return groups


def _stream(x3, init, chunk, npad, wts, run_ffn):
    (norm_mix, w_proj, consts, wba, wbb, wo, norm_ffn, wq, keys, u_bf, vt_bf, norm_final) = wts
    B, L, _ = x3.shape
    x2 = x3.reshape(B * L, D_MODEL)
    proj = _norm_proj(x2, norm_mix, w_proj, _tile(B * L, 512))
    outs = _mixers(proj.reshape(B, L, N_PROJ), x3, init, consts, (wba, wbb, wo), chunk, npad, _mixer_groups(B, chunk))
    h2, s_new, tail_a, c_new, n_new, m_new, tail_b = outs
    states = (tail_a, tail_b, s_new, c_new, n_new, m_new)
    if not run_ffn:
        return None, states
    y = _peer(h2.reshape(B * L, D_MODEL), norm_ffn, wq, keys, u_bf, vt_bf, norm_final, _tile(B * L, 512), 8)
    return y.reshape(B, L, D_MODEL), states


def kernel(x_prompt, x_sample, state_gdn_S, state_gdn_conv, state_mlstm_C, state_mlstm_n, state_mlstm_m, state_mlstm_conv, meta_tokens, norm_mix, w_in, conv_gdn, gdn_a_log, gdn_dt_bias, gdn_out_norm, conv_mlstm, mlstm_i_bias, mlstm_f_bias, mlstm_out_norm, w_branch, w_out, norm_ffn, peer_w_query, peer_sub_keys, peer_u, peer_v, norm_final):
    assert w_in.shape[0] == 1, "one trunk layer"
    w = w_in[0]
    w_proj = jnp.concatenate(
        [w[:, 0:1536], w[:, 1544:2056], w[:, 2056:3080], w[:, 3080:3592], w[:, 3600:4112], w[:, 4112:6160],
         w[:, 1536:1544], w[:, 3592:3600], jnp.zeros((D_MODEL, N_PROJ - C_SMALL - 4 * HEADS), w.dtype)],
        axis=1).astype(BF16)

    def lane_row(vec, start):
        return jnp.zeros((LANES,), F32).at[start:start + HEADS].set(vec.astype(F32))

    prm = jnp.zeros((8, LANES), F32)
    prm = prm.at[0].set(lane_row(gdn_dt_bias[0], L_A)).at[1].set(lane_row(gdn_a_log[0], L_A))
    prm = prm.at[2].set(lane_row(mlstm_i_bias[0], L_I)).at[3].set(lane_row(mlstm_f_bias[0], L_F))
    consts = (conv_gdn[0], conv_mlstm[0], prm, gdn_out_norm[0].reshape(1, DV), mlstm_out_norm[0])
    wb = w_branch[0].astype(BF16)
    wts = (norm_mix[0].reshape(1, D_MODEL), w_proj, consts, wb[:HEADS * DV], wb[HEADS * DV:], w_out[0].astype(BF16),
           norm_ffn[0].reshape(1, D_MODEL), peer_w_query[0].astype(BF16),
           peer_sub_keys[0].reshape(2 * PEER_HEADS, N_KEYS, LANES).astype(BF16),
           peer_u[0].astype(BF16), _cast_transpose(peer_v[0], _tile(peer_v.shape[1], 512)), norm_final.reshape(1, D_MODEL))

    def pad_tail(t):
        return jnp.pad(t, ((0, 0), (TAIL - (CONV_W - 1), 0), (0, 0)))

    def lanes(m):
        return jnp.broadcast_to(m[..., None], m.shape + (LANES,))

    wa = conv_gdn.shape[-1]
    wbw = conv_mlstm.shape[-1]
    zero_init = (jnp.zeros((1, TAIL, wa), F32), jnp.zeros((1, TAIL, wbw), F32),
                 jnp.zeros((1, HEADS, DV, DK), F32), jnp.zeros((1, HEADS, DV, DK), F32),
                 jnp.zeros((1, HEADS, DK), F32), jnp.zeros((1, HEADS, LANES), F32))
    x_meta = jnp.concatenate([jnp.zeros((CHUNK - N_META, D_MODEL), F32), meta_tokens.astype(F32)], axis=0)[None]
    _, meta_states = _stream(x_meta, zero_init, CHUNK, CHUNK - N_META, wts, False)
    y_p, st_p = _stream(x_prompt, meta_states, CHUNK, 0, wts, True)
    samp_init = (pad_tail(state_gdn_conv[0]), pad_tail(state_mlstm_conv[0]), state_gdn_S[0], state_mlstm_C[0],
                 state_mlstm_n[0], lanes(state_mlstm_m[0]))
    y_s, st_s = _stream(x_sample, samp_init, x_sample.shape[1], 0, wts, True)

    def pack(st):
        tail_a, tail_b, s_new, c_new, n_new, m_new = st
        return (s_new[None], tail_a[:, TAIL - (CONV_W - 1):][None], c_new[None], n_new[None], m_new[:, :, 0][None],
                tail_b[:, TAIL - (CONV_W - 1):][None])

    return (y_p, y_s) + pack(st_p) + pack(st_s)
```

```python
import functools

import jax
import jax.numpy as jnp
from jax import lax
from jax.experimental import pallas as pl
from jax.experimental.pallas import tpu as pltpu

F32 = jnp.float32
BF16 = jnp.bfloat16
HIGHEST = lax.Precision.HIGHEST

RMS_EPS = 1e-6
D_MODEL = 1024
N_META = 16
CHUNK = 64
HEADS = 4
DK = 128
DV = 128
PEER_HEADS = 8
N_KEYS = 128
PEER_TOPK = 16
LANES = 128
TAIL = 8
CONV_W = 4
VMEM_LIMIT = 56 * 1024 * 1024
MIX_MAX_GROUPS = 4
MIX_MAX_SEQS = 16
PEER_VMEM_LIMIT = 60 * 1024 * 1024
SELECT_STEPS_PER_MAIN_STEP = 12

C_QKVA, C_ZA, C_QKB, C_VB, C_OB, C_GA, C_GB, C_SMALL = 0, 1536, 2048, 3072, 3584, 4096, 5120, 6144
N_PROJ = 6272
L_A, L_B, L_I, L_F = 0, 4, 8, 12


def _rms(x, gain):
    return x * lax.rsqrt(jnp.mean(x * x, axis=-1, keepdims=True) + RMS_EPS) * gain


def _mm(a, b):
    return jnp.dot(a.astype(BF16), b.astype(BF16), preferred_element_type=F32)


def _mm_nt(a, b):
    return lax.dot_general(a.astype(BF16), b.astype(BF16), (((1,), (1,)), ((), ())), preferred_element_type=F32)


def _mm_tn(a, b):
    return lax.dot_general(a.astype(BF16), b.astype(BF16), (((0,), (0,)), ((), ())), preferred_element_type=F32)


def _mm_hi(a, b):
    return jnp.dot(a, b, preferred_element_type=F32, precision=HIGHEST)


def _norm_proj_kernel(x_ref, g_ref, w_ref, o_ref):
    o_ref[...] = jnp.dot(_rms(x_ref[...], g_ref[...]).astype(BF16), w_ref[...], preferred_element_type=F32)


def _norm_proj(x2d, gain, w_bf16, tm):
    rows = x2d.shape[0]
    return pl.pallas_call(
        _norm_proj_kernel,
        grid=(rows // tm,),
        in_specs=[pl.BlockSpec((tm, D_MODEL), lambda i: (i, 0)),
                  pl.BlockSpec((1, D_MODEL), lambda i: (0, 0)),
                  pl.BlockSpec((D_MODEL, N_PROJ), lambda i: (0, 0), pipeline_mode=pl.Buffered(1))],
        out_specs=pl.BlockSpec((tm, N_PROJ), lambda i: (i, 0)),
        out_shape=jax.ShapeDtypeStruct((rows, N_PROJ), F32),
        compiler_params=pltpu.CompilerParams(dimension_semantics=("arbitrary",), vmem_limit_bytes=VMEM_LIMIT),
        name="norm_proj",
    )(x2d, gain, w_bf16)


def _causal_conv(x, p8, w):
    xe = jnp.concatenate([p8, x], axis=0)
    y = x * w[CONV_W - 1:CONV_W]
    for k in range(1, CONV_W):
        y = y + pltpu.roll(xe, k, 0)[TAIL:] * w[CONV_W - 1 - k:CONV_W - k]
    return y, xe[x.shape[0]:]


def _mixer_kernel(C, nc, npad, NB, NBI,
                  qkva_ref, za_ref, qkb_ref, vb_ref, og_ref, sm_ref,
                  ta0_ref, tb0_ref, s0_ref, c0_ref, n0_ref, m0_ref,
                  cwa_ref, cwb_ref, prm_ref, gnorm_ref, mnorm_ref,
                  ga_ref, gb_ref, h_ref, wba_ref, wbb_ref, wo_ref,
                  h2_ref, so_ref, tao_ref, co_ref, no_ref, mo_ref, tbo_ref,
                  s_sc, c_sc, n_sc, m_sc, pa_sc, pb_sc):
    ci = pl.program_id(1)
    branch_a, branch_b = {}, {}
    GS = (2 * LANES) // (HEADS * C)
    NBLK = GS * HEADS
    R = NBLK * C
    T = GS * C
    log2c = C.bit_length() - 1

    def rows(b):
        return slice(b * C, (b + 1) * C)

    def stack(pieces):
        return jnp.concatenate(pieces, axis=0)

    def wide(x):
        return jnp.concatenate([x] * (R // LANES), axis=1)

    @pl.when(ci == 0)
    def _init():
        for n in range(NB):
            ni = min(n, NBI - 1)
            s_sc[n] = s0_ref[ni]
            c_sc[n] = c0_ref[ni]
            n_sc[n] = n0_ref[ni]
            pa_sc[n] = ta0_ref[ni]
            pb_sc[n] = tb0_ref[ni]
            for h in range(HEADS):
                m_sc[rows(n * HEADS + h), :] = jnp.broadcast_to(m0_ref[ni, h:h + 1, :], (C, LANES))

    def group(grp):
        seqs = list(range(grp * GS, (grp + 1) * GS))
        blocks = [(n, h) for n in seqs for h in range(HEADS)]
        mrows = slice(grp * R, (grp + 1) * R)
        rowi = lax.broadcasted_iota(jnp.int32, (R, R), 0)
        coli = lax.broadcasted_iota(jnp.int32, (R, R), 1)
        same = (rowi >> log2c) == (coli >> log2c)
        incl = same & (rowi >= coli)
        strict = same & (rowi > coli)
        neg_inf = -jnp.inf

        qkva, qkb = {}, {}
        for n in seqs:
            ya, tail_a = _causal_conv(qkva_ref[n], pa_sc[n], cwa_ref[...])
            yb, tail_b = _causal_conv(qkb_ref[n], pb_sc[n], cwb_ref[...])
            pa_sc[n] = tail_a
            pb_sc[n] = tail_b
            qkva[n] = jax.nn.silu(ya)
            qkb[n] = jax.nn.silu(yb)
        hd = HEADS * DK
        q_all = stack([qkva[n][:, h * DK:(h + 1) * DK] for n, h in blocks])
        k_all = stack([qkva[n][:, hd + h * DK:hd + (h + 1) * DK] for n, h in blocks])
        v_all = stack([qkva[n][:, 2 * hd + h * DV:2 * hd + (h + 1) * DV] for n, h in blocks])
        qb_all = stack([qkb[n][:, h * DK:(h + 1) * DK] for n, h in blocks])
        kb_all = stack([qkb[n][:, hd + h * DK:hd + (h + 1) * DK] for n, h in blocks]) * (DK ** -0.5)
        vb_all = stack([vb_ref[n, :, h * DV:(h + 1) * DV] for n, h in blocks])

        sm = stack([sm_ref[n] for n in seqs])
        lane = lax.broadcasted_iota(jnp.int32, (T, LANES), 1)
        g_full = -jnp.exp(prm_ref[1:2]) * jax.nn.softplus(sm + prm_ref[0:1])
        beta_full = jax.nn.sigmoid(sm)
        logi_full = sm + prm_ref[2:3]
        logf_full = jax.nn.log_sigmoid(sm + prm_ref[3:4])
        if npad:
            real = (lax.broadcasted_iota(jnp.int32, (T, LANES), 0) & (C - 1)) >= npad
            g_full = jnp.where(real, g_full, 0.0)
            beta_full = jnp.where(real, beta_full, 0.0)
            logf_full = jnp.where(real, logf_full, 0.0)
        cs_in = jnp.where(lane < L_A + HEADS, g_full,
                          jnp.where((lane >= L_F) & (lane < L_F + HEADS), logf_full, 0.0))
        ti = lax.broadcasted_iota(jnp.int32, (T, T), 0)
        tj = lax.broadcasted_iota(jnp.int32, (T, T), 1)
        tril = (((ti >> log2c) == (tj >> log2c)) & (ti >= tj)).astype(F32)
        cums = _mm_hi(tril, cs_in)
        yield
        gates = jnp.where((lane >= L_B) & (lane < L_B + HEADS), beta_full,
                          jnp.where((lane >= L_I) & (lane < L_I + HEADS), logi_full, cums))
        last = stack([jnp.broadcast_to(cums[k * C + C - 1:k * C + C, :], (C, LANES)) for k in range(GS)])

        def col(src, base):
            return stack([jnp.broadcast_to(src[(n - seqs[0]) * C:(n - seqs[0] + 1) * C, base + h:base + h + 1], (C, LANES))
                          for n, h in blocks])

        g_c, beta_c, li_c, f_c = col(gates, L_A), col(gates, L_B), col(gates, L_I), col(gates, L_F)
        glast_c, fend_c = col(last, L_A), col(last, L_F)
        lane_r = lax.broadcasted_iota(jnp.int32, (R, LANES), 1)
        rowform = jnp.where(lane_r == 0, g_c, jnp.where(lane_r == 1, li_c, f_c)).T
        g_r, li_r, f_r = rowform[0:1, :], rowform[1:2, :], rowform[2:3, :]
        if npad:
            li_c = jnp.where((lax.broadcasted_iota(jnp.int32, (R, LANES), 0) & (C - 1)) >= npad, li_c, neg_inf)
            li_r = jnp.where((coli[0:1, :] & (C - 1)) >= npad, li_r, neg_inf)

        q_all = q_all * lax.rsqrt(jnp.sum(q_all * q_all, axis=-1, keepdims=True) + RMS_EPS) * (DK ** -0.5)
        k_all = k_all * lax.rsqrt(jnp.sum(k_all * k_all, axis=-1, keepdims=True) + RMS_EPS)
        decay = jnp.exp(jnp.where(incl, wide(g_c) - g_r, neg_inf))
        low = jnp.where(strict, wide(beta_c) * decay * _mm_nt(k_all, k_all), 0.0)
        yield
        eye = (rowi == coli).astype(F32)
        xinv = eye - jnp.where(((rowi & 1) == 1) & (coli == rowi - 1), low, 0.0)
        shift = 1
        while (1 << shift) < C:
            br = rowi >> shift
            bc = coli >> shift
            cm = jnp.where(((br & 1) == 1) & (bc == br - 1), low, 0.0)
            t = _mm(xinv, cm)
            yield
            xinv = xinv - _mm(t, xinv)
            yield
            shift += 1
        eg = jnp.exp(g_c)
        u_all = _mm(xinv, beta_c * v_all)
        w_all = _mm(xinv, (beta_c * eg) * k_all)
        yield
        qk = _mm_nt(q_all, k_all) * decay
        yield
        q_dec = q_all * eg
        k_end = k_all * jnp.exp(glast_c - g_c)
        g_end = jnp.exp(glast_c)
        deltas, inter = [], []
        for b, (n, h) in enumerate(blocks):
            s = s_sc[n, h]
            delta = u_all[rows(b)] - _mm_nt(w_all[rows(b)], s)
            inter.append(_mm_nt(q_dec[rows(b)], s))
            s_sc[n, h] = g_end[b * C:b * C + 1, :] * s + _mm_tn(delta, k_end[rows(b)])
            deltas.append(delta)
            yield
        o_all = stack(inter) + _mm(qk, stack(deltas))
        yield
        za_all = stack([za_ref[n, :, h * DV:(h + 1) * DV] for n, h in blocks])
        o_all = _rms(o_all, gnorm_ref[...]) * jax.nn.silu(za_all)
        for b, (n, h) in enumerate(blocks):
            branch_a[n, h] = o_all[rows(b)]

        dmat = jnp.where(incl, wide(f_c) - f_r + li_r, neg_inf)
        dmax = jnp.broadcast_to(jnp.max(dmat, axis=1, keepdims=True), (R, LANES))
        qkm = _mm_nt(qb_all, kb_all)
        yield
        ms = m_sc[mrows, :]
        m_inter = f_c + ms
        m_t = jnp.maximum(m_inter, dmax)
        w_intra = jnp.exp(dmat - wide(m_t)) * qkm
        w_inter = jnp.exp(m_inter - m_t)
        e_col = fend_c - f_c + li_c
        e_max = jnp.max(e_col.reshape(NBLK, C, LANES), axis=1, keepdims=True)
        e_max = jnp.broadcast_to(e_max, (NBLK, C, LANES)).reshape(R, LANES)
        m_new = jnp.maximum(fend_c + ms, e_max)
        dec = jnp.exp(fend_c + ms - m_new)
        wk = jnp.exp(e_col - m_new)
        wkv = wk * vb_all
        wkk = wk * kb_all
        inter, ns_rows = [], []
        for b, (n, h) in enumerate(blocks):
            cs = c_sc[n, h]
            ns = n_sc[n, h:h + 1, :]
            inter.append(_mm_nt(qb_all[rows(b)], cs))
            ns_rows.append(jnp.broadcast_to(ns, (C, LANES)))
            dec_b = dec[b * C:b * C + 1, :]
            c_sc[n, h] = dec_b * cs + _mm_tn(wkv[rows(b)], kb_all[rows(b)])
            n_sc[n, h:h + 1, :] = dec_b * ns + jnp.sum(wkk[rows(b)], axis=0, keepdims=True)
            yield
        m_sc[mrows, :] = m_new
        yield
        num = w_inter * stack(inter) + _mm(w_intra, vb_all)
        den = (w_inter[:, 0:1] * jnp.sum(qb_all * stack(ns_rows), axis=1, keepdims=True)
               + jnp.sum(w_intra, axis=1, keepdims=True))
        hh = num / jnp.maximum(jnp.abs(den), jnp.exp(-m_t[:, 0:1]))
        mnorm = stack([jnp.broadcast_to(mnorm_ref[h:h + 1, :], (C, DV)) for _, h in blocks])
        og_all = stack([og_ref[n, :, h * DV:(h + 1) * DV] for n, h in blocks])
        hh = _rms(hh, mnorm) * jax.nn.sigmoid(og_all)
        for b, (n, h) in enumerate(blocks):
            branch_b[n, h] = hh[rows(b)]

    gens = [group(grp) for grp in range(NB // GS)]
    while gens:
        for gen in list(gens):
            try:
                next(gen)
            except StopIteration:
                gens.remove(gen)

    def seq_rows(parts):
        return stack([jnp.concatenate([parts[n, h] for h in range(HEADS)], axis=1) for n in range(NB)]).astype(BF16)

    ya = jnp.dot(seq_rows(branch_a), wba_ref[...], preferred_element_type=F32)
    yb = jnp.dot(seq_rows(branch_b), wbb_ref[...], preferred_element_type=F32)
    merged = (jax.nn.sigmoid(stack([ga_ref[n] for n in range(NB)])) * ya
              + jax.nn.sigmoid(stack([gb_ref[n] for n in range(NB)])) * yb)
    h2 = stack([h_ref[n] for n in range(NB)]) + jnp.dot(merged.astype(BF16), wo_ref[...], preferred_element_type=F32)
    for n in range(NB):
        h2_ref[n] = h2[n * C:(n + 1) * C]

    @pl.when(ci == nc - 1)
    def _fin():
        so_ref[...] = s_sc[...]
        co_ref[...] = c_sc[...]
        no_ref[...] = n_sc[...]
        tao_ref[...] = pa_sc[...]
        tbo_ref[...] = pb_sc[...]
        for n in range(NB):
            for h in range(HEADS):
                b = n * HEADS + h
                mo_ref[n, h:h + 1, :] = m_sc[b * C:b * C + 1, :]


def _mixers(proj3, h3, init, consts, out_w, C, npad, groups):
    B, L, _ = proj3.shape
    nc = L // C
    assert C & (C - 1) == 0 and (2 * LANES) % (HEADS * C) == 0, "chunk must stack into 2*LANES rows"
    NB = groups * ((2 * LANES) // (HEADS * C))
    assert B % NB == 0
    cwa, cwb, prm, gnorm, mnorm = consts
    wba, wbb, wo = out_w
    wa = cwa.shape[1]
    wb = cwb.shape[1]
    tail_a, tail_b, s0, c0, n0, m0 = init
    shared_init = s0.shape[0] == 1 and B > 1
    assert shared_init or s0.shape[0] == B
    NBI = 1 if shared_init else NB
    wh = HEADS * DV

    def col(width, start):
        return pl.BlockSpec((NB, C, width), lambda b, c, _s=start // width: (b, c, _s))

    def st(shape):
        nd = len(shape)
        if shared_init:
            return pl.BlockSpec((1,) + shape, lambda b, c, _n=nd: (0,) * (_n + 1))
        return pl.BlockSpec((NB,) + shape, lambda b, c, _n=nd: (b,) + (0,) * _n)

    def full(a):
        return pl.BlockSpec(a.shape, lambda b, c, _n=a.ndim: (0,) * _n)

    def outst(shape):
        return pl.BlockSpec((NB,) + shape, lambda b, c, _n=len(shape): (b,) + (0,) * _n)

    in_specs = [col(wa, C_QKVA), col(wh, C_ZA), col(wb, C_QKB), col(wh, C_VB), col(wh, C_OB), col(LANES, C_SMALL),
                st((TAIL, wa)), st((TAIL, wb)), st((HEADS, DV, DK)), st((HEADS, DV, DK)), st((HEADS, DK)), st((HEADS, LANES)),
                full(cwa), full(cwb), full(prm), full(gnorm), full(mnorm),
                col(D_MODEL, C_GA), col(D_MODEL, C_GB), pl.BlockSpec((NB, C, D_MODEL), lambda b, c: (b, c, 0)),
                full(wba), full(wbb), full(wo)]
    out_specs = [pl.BlockSpec((NB, C, D_MODEL), lambda b, c: (b, c, 0)),
                 outst((HEADS, DV, DK)), outst((TAIL, wa)), outst((HEADS, DV, DK)), outst((HEADS, DK)),
                 outst((HEADS, LANES)), outst((TAIL, wb))]
    out_shape = [jax.ShapeDtypeStruct((B, L, D_MODEL), F32),
                 jax.ShapeDtypeStruct((B, HEADS, DV, DK), F32), jax.ShapeDtypeStruct((B, TAIL, wa), F32),
                 jax.ShapeDtypeStruct((B, HEADS, DV, DK), F32), jax.ShapeDtypeStruct((B, HEADS, DK), F32),
                 jax.ShapeDtypeStruct((B, HEADS, LANES), F32), jax.ShapeDtypeStruct((B, TAIL, wb), F32)]
    scratch = [pltpu.VMEM((NB, HEADS, DV, DK), F32), pltpu.VMEM((NB, HEADS, DV, DK), F32),
               pltpu.VMEM((NB, HEADS, DK), F32), pltpu.VMEM((NB * HEADS * C, LANES), F32),
               pltpu.VMEM((NB, TAIL, wa), F32), pltpu.VMEM((NB, TAIL, wb), F32)]
    return pl.pallas_call(
        functools.partial(_mixer_kernel, C, nc, npad, NB, NBI),
        grid=(B // NB, nc),
        in_specs=in_specs, out_specs=out_specs, out_shape=out_shape, scratch_shapes=scratch,
        compiler_params=pltpu.CompilerParams(dimension_semantics=("arbitrary", "arbitrary"),
                                             vmem_limit_bytes=VMEM_LIMIT),
        name="mixers",
    )(proj3, proj3, proj3, proj3, proj3, proj3, tail_a, tail_b, s0, c0, n0, m0, cwa, cwb, prm, gnorm, mnorm,
      proj3, proj3, h3, wba, wbb, wo)


N_RANK = PEER_TOPK + 1
_STAIR = [(k, N_RANK // (k + 1)) for k in range(N_RANK)]
_N_CAND = sum(n for _, n in _STAIR)
_CAND_ROWS = -(-_N_CAND // 8) * 8


def _sort16_network():
    pairs = []

    def merge(lo, n, r):
        step = r * 2
        if step < n:
            merge(lo, n, step)
            merge(lo + r, n, step)
            for i in range(lo + r, lo + n - r, step):
                pairs.append((i, i + r))
        else:
            pairs.append((lo, lo + r))

    def sort(lo, n):
        if n > 1:
            m = n // 2
            sort(lo, m)
            sort(lo + m, m)
            merge(lo, n, 1)

    sort(0, 16)
    return pairs


_SORT16 = _sort16_network()


def _top_values(x):
    tiles = [x[8 * v:8 * v + 8, :] for v in range(N_KEYS // 8)]
    for a, b in _SORT16:
        hi = jnp.maximum(tiles[a], tiles[b])
        tiles[b] = jnp.minimum(tiles[a], tiles[b])
        tiles[a] = hi
    yield
    vals = []
    for kk in range(N_RANK):
        mx = jnp.max(tiles[0], axis=0, keepdims=True)
        vals.append(mx)
        hit = tiles[0] == mx
        depth = min(len(tiles) - 1, N_RANK - 1 - kk)
        for v in range(depth):
            tiles[v] = jnp.where(hit, tiles[v + 1], tiles[v])
        if depth == len(tiles) - 1:
            tiles[depth] = jnp.where(hit, -jnp.inf, tiles[depth])
        yield
    return vals


def _peer_select_steps(TB, hd, q_sc, keys_ref, gates):
    r1_sc, e1_sc, cnt_sc, e0_sc = gates
    neg_inf = -jnp.inf
    s_lists = [lax.dot_general(keys_ref[2 * hd + p], q_sc[2 * hd + p], (((1,), (1,)), ((), ())),
                               preferred_element_type=F32) for p in range(2)]
    yield
    rank1_t, e1_t, cnt_t, e0_t = [], [], [], []
    for lt in range(TB // LANES):
        tok = slice(lt * LANES, (lt + 1) * LANES)
        s0 = s_lists[0][:, tok]
        s1 = s_lists[1][:, tok]
        top1 = yield from _top_values(s1)
        top0 = yield from _top_values(s0)
        rank1 = jnp.full((N_KEYS, LANES), float(N_KEYS - 1), F32)
        for kk in reversed(range(N_RANK)):
            rank1 = jnp.where(s1 >= top1[kk], float(kk), rank1)
            if kk % 4 == 0:
                yield
        top1_arr = jnp.concatenate(top1, axis=0)
        cand = jnp.concatenate([top0[kk] + top1_arr[0:n] for kk, n in _STAIR]
                               + [jnp.full((_CAND_ROWS - _N_CAND, LANES), neg_inf, F32)], axis=0)
        top = top0[0] + top1[0]
        z = jnp.zeros((1, LANES), F32)
        tau = top
        x = cand
        for kk in range(PEER_TOPK):
            tau = jnp.max(x, axis=0, keepdims=True)
            z = z + jnp.exp(tau - top)
            x = jnp.where(x == tau, neg_inf, x)
            if kk % 4 == 3:
                yield
        tau = 0.5 * (tau + jnp.max(x, axis=0, keepdims=True))
        thr = tau - s0
        cnt = jnp.zeros((N_KEYS, LANES), F32)
        for kk in range(PEER_TOPK):
            cnt = jnp.where(top1[kk] >= thr, float(kk + 1), cnt)
            if kk % 4 == 3:
                yield
        rank1_t.append(rank1.astype(BF16))
        e1_t.append(jnp.exp(s1 - top1[0]).astype(BF16))
        cnt_t.append(cnt)
        e0_t.append(jnp.exp(s0 - top0[0]) / z)
        yield
    r1_sc[hd] = jnp.concatenate(rank1_t, axis=1)
    e1_sc[hd] = jnp.concatenate(e1_t, axis=1)
    cnt_sc[hd] = jnp.concatenate(cnt_t, axis=1)
    e0_sc[hd] = jnp.concatenate(e0_t, axis=1)


def _peer_kernel(TB, NI, NBLK, *refs):
    (h_ref, hn_ref, gf_ref, wq_ref, keys_ref, u0_ref, ua_ref, ub_ref, vt_ref, gfin_ref, o_ref,
     xn0_sc, xn1_sc, q_sc, r1a_sc, e1a_sc, cnta_sc, e0a_sc, r1b_sc, e1b_sc, cntb_sc, e0b_sc,
     acc_sc, act0_sc, act1_sc) = refs
    i = pl.program_id(0)
    g = pl.program_id(1)
    ng = pl.num_programs(1)
    EB = NI * N_KEYS
    sets = ((xn0_sc, (r1a_sc, e1a_sc, cnta_sc, e0a_sc)), (xn1_sc, (r1b_sc, e1b_sc, cntb_sc, e0b_sc)))
    act_bufs = (act0_sc, act1_sc)
    u_next = (ua_ref, ub_ref)

    def pre_act(u_blk, xn_sc):
        return lax.dot_general(u_blk, xn_sc[...], (((1,), (1,)), ((), ())), preferred_element_type=F32)

    def project(h, xn_sc):
        xn = _rms(h, gf_ref[...]).astype(BF16)
        xn_sc[...] = xn
        q = jnp.dot(xn, wq_ref[...], preferred_element_type=F32)
        for j in range(2 * PEER_HEADS):
            q_sc[j] = q[:, j * LANES:(j + 1) * LANES].astype(BF16)

    @pl.when((i == 0) & (g == 0))
    def _first_block():
        project(h_ref[...], xn0_sc)

        def head(hd, carry):
            for _ in _peer_select_steps(TB, hd, q_sc, keys_ref, sets[0][1]):
                pass
            return carry

        lax.fori_loop(0, PEER_HEADS, head, 0)

    def body(par):
        xn_cur, (r1_sc, e1_sc, cnt_sc, e0_sc) = sets[par]
        xn_nxt, gates_nxt = sets[1 - par]

        @pl.when(g == 0)
        def _start():
            acc_sc[...] = jnp.zeros_like(acc_sc)
            act0_sc[...] = pre_act(u0_ref[...], xn_cur)
            project(hn_ref[...], xn_nxt)

        def main_steps():
            for sub in range(2):
                cur, nxt = sub, 1 - sub
                act_bufs[nxt][...] = pre_act(u_next[sub][...], xn_cur)
                yield
                parts = []
                for ii in range(0, NI, 2):
                    i_glob = (2 * g + sub) * NI + ii
                    wgts = [jnp.zeros((N_KEYS, TB), BF16), jnp.zeros((N_KEYS, TB), BF16)]
                    for hd in range(PEER_HEADS):
                        r1 = r1_sc[hd]
                        e1 = e1_sc[hd]
                        for d in range(2):
                            cnt = cnt_sc[hd, pl.ds(i_glob + d, 1), :].astype(BF16)
                            e0 = e0_sc[hd, pl.ds(i_glob + d, 1), :].astype(BF16)
                            wgts[d] = wgts[d] + jnp.where(r1 < cnt, e1, jnp.zeros((), BF16)) * e0
                    for d in range(2):
                        act = act_bufs[cur][(ii + d) * N_KEYS:(ii + d + 1) * N_KEYS, :]
                        act = 0.5 * act * (1.0 + lax.erf(act * (2.0 ** -0.5)))
                        parts.append(act.astype(BF16) * wgts[d])
                    yield
                acc_sc[...] += jnp.dot(vt_ref[:, sub * EB:(sub + 1) * EB], jnp.concatenate(parts, axis=0),
                                       preferred_element_type=F32)
                yield

        select = _peer_select_steps(TB, g, q_sc, keys_ref, gates_nxt)
        main = main_steps()
        next(select)
        main_live = select_live = True
        while main_live or select_live:
            if main_live:
                main_live = next(main, "done") != "done"
            for _ in range(SELECT_STEPS_PER_MAIN_STEP):
                if select_live:
                    select_live = next(select, "done") != "done"

        @pl.when(g == ng - 1)
        def _fin():
            o_ref[...] = _rms(h_ref[...] + acc_sc[...].T, gfin_ref[...])

    for par in range(2):
        pl.when(i % 2 == par)(functools.partial(body, par))


def _peer(h2d, gf, wq, keys, u_bf, vt_bf, gfin, tb, ni):
    rows = h2d.shape[0]
    n_exp = u_bf.shape[0]
    eb = ni * N_KEYS
    nblk = n_exp // eb
    ntb = rows // tb
    assert nblk == 2 * PEER_HEADS, "two expert blocks and one selection head per grid step"
    once = pl.Buffered(1)
    gate_bf = pltpu.VMEM((PEER_HEADS, N_KEYS, tb), BF16)
    gate_f32 = pltpu.VMEM((PEER_HEADS, N_KEYS, tb), F32)
    return pl.pallas_call(
        functools.partial(_peer_kernel, tb, ni, nblk),
        grid=(ntb, nblk // 2),
        in_specs=[pl.BlockSpec((tb, D_MODEL), lambda i, e: (i, 0)),
                  pl.BlockSpec((tb, D_MODEL), lambda i, e: (jnp.minimum(i + 1, ntb - 1), 0), pipeline_mode=once),
                  pl.BlockSpec((1, D_MODEL), lambda i, e: (0, 0)),
                  pl.BlockSpec(wq.shape, lambda i, e: (0, 0), pipeline_mode=once),
                  pl.BlockSpec(keys.shape, lambda i, e: (0, 0, 0), pipeline_mode=once),
                  pl.BlockSpec((eb, D_MODEL), lambda i, e: (0, 0), pipeline_mode=once),
                  pl.BlockSpec((eb, D_MODEL), lambda i, e: (jnp.minimum(2 * e + 1, nblk - 1), 0)),
                  pl.BlockSpec((eb, D_MODEL), lambda i, e: (jnp.minimum(2 * e + 2, nblk - 1), 0)),
                  pl.BlockSpec((D_MODEL, 2 * eb), lambda i, e: (0, e)),
                  pl.BlockSpec((1, D_MODEL), lambda i, e: (0, 0))],
        out_specs=pl.BlockSpec((tb, D_MODEL), lambda i, e: (i, 0)),
        out_shape=jax.ShapeDtypeStruct((rows, D_MODEL), F32),
        scratch_shapes=[pltpu.VMEM((tb, D_MODEL), BF16), pltpu.VMEM((tb, D_MODEL), BF16),
                        pltpu.VMEM((2 * PEER_HEADS, tb, LANES), BF16),
                        gate_bf, gate_bf, gate_f32, gate_f32, gate_bf, gate_bf, gate_f32, gate_f32,
                        pltpu.VMEM((D_MODEL, tb), F32),
                        pltpu.VMEM((eb, tb), F32), pltpu.VMEM((eb, tb), F32)],
        compiler_params=pltpu.CompilerParams(dimension_semantics=("arbitrary", "arbitrary"),
                                             vmem_limit_bytes=PEER_VMEM_LIMIT),
        name="peer",
    )(h2d, h2d, gf, wq, keys, u_bf, u_bf, u_bf, vt_bf, gfin)


def _cast_transpose_kernel(x_ref, o_ref):
    o_ref[...] = x_ref[...].T.astype(BF16)


def _cast_transpose(x, tr):
    n, d = x.shape
    return pl.pallas_call(
        _cast_transpose_kernel,
        grid=(n // tr,),
        in_specs=[pl.BlockSpec((tr, d), lambda i: (i, 0))],
        out_specs=pl.BlockSpec((d, tr), lambda i: (0, i)),
        out_shape=jax.ShapeDtypeStruct((d, n), BF16),
        compiler_params=pltpu.CompilerParams(dimension_semantics=("arbitrary",), vmem_limit_bytes=VMEM_LIMIT),
        name="cast_transpose",
    )(x)


def _tile(rows, cap):
    t = cap
    while t > 8 and rows % t:
        t //= 2
    return t if rows % t == 0 else rows


def _mixer_groups(batch, chunk):
    per_group = (2 * LANES) // (HEADS * chunk)
    groups = MIX_MAX_GROUPS
    while groups > 1 and (batch % (groups * per_group) or groups * per_group > MIX_MAX_SEQS):
        groups //= 2
    return groups


def _stream(x3, init, chunk, npad, wts, run_ffn):
    (norm_mix, w_proj, consts, wba, wbb, wo, norm_ffn, wq, keys, u_bf, vt_bf, norm_final) = wts
    B, L, _ = x3.shape
    x2 = x3.reshape(B * L, D_MODEL)
    proj = _norm_proj(x2, norm_mix, w_proj, _tile(B * L, 512))
    outs = _mixers(proj.reshape(B, L, N_PROJ), x3, init, consts, (wba, wbb, wo), chunk, npad, _mixer_groups(B, chunk))
    h2, s_new, tail_a, c_new, n_new, m_new, tail_b = outs
    states = (tail_a, tail_b, s_new, c_new, n_new, m_new)
    if not run_ffn:
        return None, states
    y = _peer(h2.reshape(B * L, D_MODEL), norm_ffn, wq, keys, u_bf, vt_bf, norm_final, _tile(B * L, 512), 8)
    return y.reshape(B, L, D_MODEL), states


def kernel(x_prompt, x_sample, state_gdn_S, state_gdn_conv, state_mlstm_C, state_mlstm_n, state_mlstm_m, state_mlstm_conv, meta_tokens, norm_mix, w_in, conv_gdn, gdn_a_log, gdn_dt_bias, gdn_out_norm, conv_mlstm, mlstm_i_bias, mlstm_f_bias, mlstm_out_norm, w_branch, w_out, norm_ffn, peer_w_query, peer_sub_keys, peer_u, peer_v, norm_final):
    assert w_in.shape[0] == 1, "one trunk layer"
    w = w_in[0]
    w_proj = jnp.concatenate(
        [w[:, 0:1536], w[:, 1544:2056], w[:, 2056:3080], w[:, 3080:3592], w[:, 3600:4112], w[:, 4112:6160],
         w[:, 1536:1544], w[:, 3592:3600], jnp.zeros((D_MODEL, N_PROJ - C_SMALL - 4 * HEADS), w.dtype)],
        axis=1).astype(BF16)

    def lane_row(vec, start):
        return jnp.zeros((LANES,), F32).at[start:start + HEADS].set(vec.astype(F32))

    prm = jnp.zeros((8, LANES), F32)
    prm = prm.at[0].set(lane_row(gdn_dt_bias[0], L_A)).at[1].set(lane_row(gdn_a_log[0], L_A))
    prm = prm.at[2].set(lane_row(mlstm_i_bias[0], L_I)).at[3].set(lane_row(mlstm_f_bias[0], L_F))
    consts = (conv_gdn[0], conv_mlstm[0], prm, gdn_out_norm[0].reshape(1, DV), mlstm_out_norm[0])
    wb = w_branch[0].astype(BF16)
    wts = (norm_mix[0].reshape(1, D_MODEL), w_proj, consts, wb[:HEADS * DV], wb[HEADS * DV:], w_out[0].astype(BF16),
           norm_ffn[0].reshape(1, D_MODEL), peer_w_query[0].astype(BF16),
           peer_sub_keys[0].reshape(2 * PEER_HEADS, N_KEYS, LANES).astype(BF16),
           peer_u[0].astype(BF16), _cast_transpose(peer_v[0], _tile(peer_v.shape[1], 512)), norm_final.reshape(1, D_MODEL))

    def pad_tail(t):
        return jnp.pad(t, ((0, 0), (TAIL - (CONV_W - 1), 0), (0, 0)))

    def lanes(m):
        return jnp.broadcast_to(m[..., None], m.shape + (LANES,))

    wa = conv_gdn.shape[-1]
    wbw = conv_mlstm.shape[-1]
    zero_init = (jnp.zeros((1, TAIL, wa), F32), jnp.zeros((1, TAIL, wbw), F32),
                 jnp.zeros((1, HEADS, DV, DK), F32), jnp.zeros((1, HEADS, DV, DK), F32),
                 jnp.zeros((1, HEADS, DK), F32), jnp.zeros((1, HEADS, LANES), F32))
    x_meta = jnp.concatenate([jnp.zeros((CHUNK - N_META, D_MODEL), F32), meta_tokens.astype(F32)], axis=0)[None]
    _, meta_states = _stream(x_meta, zero_init, CHUNK, CHUNK - N_META, wts, False)
    y_p, st_p = _stream(x_prompt, meta_states, CHUNK, 0, wts, True)
    samp_init = (pad_tail(state_gdn_conv[0]), pad_tail(state_mlstm_conv[0]), state_gdn_S[0], state_mlstm_C[0],
                 state_mlstm_n[0], lanes(state_mlstm_m[0]))
    y_s, st_s = _stream(x_sample, samp_init, x_sample.shape[1], 0, wts, True)

    def pack(st):
        tail_a, tail_b, s_new, c_new, n_new, m_new = st
        return (s_new[None], tail_a[:, TAIL - (CONV_W - 1):][None], c_new[None], n_new[None], m_new[:, :, 0][None],
                tail_b[:, TAIL - (CONV_W - 1):][None])

    return (y_p, y_s) + pack(st_p) + pack(st_s)
```
